```python
import functools
import jax, jax.numpy as jnp
from jax import lax
import numpy as np

D_MODEL = 1024
BATCH = 8
SEQ = 2048
DEPTH = 1
DEC_BATCH = 32
DEC_SEQ = 1
PAST_LEN = 8192
PAGE_SIZE = 128

MIX_WIDTH = D_MODEL
H_RET = 8
D_RET = MIX_WIDTH // (2 * H_RET)
H_ATT = 8
D_HEAD = MIX_WIDTH // (2 * H_ATT)
RET_W = H_RET * D_RET
ATT_W = H_ATT * D_HEAD
H_IDX = 8
D_IDX = 64
TOPK_MAX = 256
N_EXPERTS = 32
TOP_K = 4
D_FF = D_MODEL
SWIGLU_LIMIT = 7.0
SWIGLU_ALPHA = 1.702
ROPE_THETA = 10000.0
EPS = 1e-6
CHUNK = 128
Q_BLOCK = 128
IN_COLS = 4 * RET_W + 3 * ATT_W + H_IDX * D_IDX + D_IDX + H_IDX

kernel_name = 'hymba_retention_dsa_moe_adaln_step'

F32 = jnp.float32


def rms_norm(x, w):
    xf = x.astype(F32)
    y = xf * lax.rsqrt(jnp.mean(xf * xf, axis=-1, keepdims=True) + EPS)
    return (y * w.astype(F32)).astype(x.dtype)


def rope(x, pos):
    d = x.shape[-1]
    inv = jnp.power(ROPE_THETA, -jnp.arange(0, d, 2, dtype=F32) / d)
    ang = pos.astype(F32)[:, None] * inv[None, :]
    ang = jnp.concatenate([ang, ang], axis=-1)
    shape = (1, pos.shape[0]) + (1,) * (x.ndim - 3) + (d,)
    cos = jnp.cos(ang).reshape(shape)
    sin = jnp.sin(ang).reshape(shape)
    xf = x.astype(F32)
    rot = jnp.concatenate([-xf[..., d // 2:], xf[..., :d // 2]], axis=-1)
    return (xf * cos + rot * sin).astype(x.dtype)


def retention_chunks(q, k, v, s0):
    B, T, H, _ = q.shape
    chunk = CHUNK if T % CHUNK == 0 else T
    n = T // chunk
    log_g = jnp.log1p(-jnp.power(2.0, -5.0 - jnp.arange(H, dtype=F32)))
    idx = jnp.arange(chunk, dtype=F32)
    rel = idx[:, None] - idx[None, :]
    decay = jnp.where(rel[None] >= 0, jnp.exp(jnp.maximum(rel, 0.0)[None] * log_g[:, None, None]), 0.0)
    q_dec = jnp.exp((idx + 1.0)[:, None] * log_g[None, :])
    k_dec = jnp.exp((chunk - 1.0 - idx)[:, None] * log_g[None, :])
    g_chunk = jnp.exp(chunk * log_g)

    def to_chunks(a):
        return a.astype(F32).reshape(B, n, chunk, H, a.shape[-1]).transpose(1, 0, 2, 3, 4)

    def step(S, inp):
        qc, kc, vc = inp
        inner = jnp.einsum('bihd,bjhd->bhij', qc, kc) * decay[None]
        o = (jnp.einsum('bhij,bjhv->bihv', inner, vc)
             + jnp.einsum('bihd,bhdv->bihv', qc, S) * q_dec[None, :, :, None])
        S = S * g_chunk[None, :, None, None] + jnp.einsum('bjhd,bjhv->bhdv', kc * k_dec[None, :, :, None], vc)
        return S, o

    S, o = lax.scan(step, s0.astype(F32), (to_chunks(q), to_chunks(k), to_chunks(v)))
    o = o.transpose(1, 0, 2, 3, 4).reshape(B, T, H, -1)
    return o, S


def retention_group(rq, rk, rv, rg, pos, s0, gn_w):
    B, T, _ = rq.shape
    q = rope(rq.reshape(B, T, H_RET, D_RET), pos)
    k = rope(rk.reshape(B, T, H_RET, D_RET), pos) * (D_RET ** -0.5)
    v = rv.reshape(B, T, H_RET, D_RET)
    o, S = retention_chunks(q, k, v, s0)
    mu = jnp.mean(o, axis=-1, keepdims=True)
    var = jnp.mean((o - mu) ** 2, axis=-1, keepdims=True)
    on = ((o - mu) * lax.rsqrt(var + EPS)).reshape(B, T, RET_W) * gn_w.astype(F32)
    out = jax.nn.silu(rg.astype(F32)) * on
    return out.astype(rq.dtype), S.astype(rq.dtype)


def index_scores(qi, wi, ki):
    s = jnp.einsum('bqhd,bsd->bqhs', qi.astype(F32), ki.astype(F32)) * (D_IDX ** -0.5)
    return jnp.einsum('bqhs,bqh->bqs', jax.nn.relu(s), wi.astype(F32))


def indexer_topk(scores, q_pos, n_keys, k_top):
    key_pos = jnp.arange(n_keys)
    visible = key_pos[None, None, :] <= q_pos[None, :, None]
    scores = jnp.where(visible, scores, -jnp.inf)
    _, idx = lax.top_k(scores, k_top)
    ok = idx <= q_pos[None, :, None]
    return idx, ok


def sparse_attend(q, k_sel, v_sel, ok):
    s = jnp.einsum('bqhd,bqkhd->bqhk', q.astype(F32), k_sel.astype(F32)) * (D_HEAD ** -0.5)
    s = jnp.where(ok[:, :, None, :], s, -jnp.inf)
    p = jax.nn.softmax(s, axis=-1)
    return jnp.einsum('bqhk,bqkhd->bqhd', p, v_sel.astype(F32)).astype(q.dtype)


def dsa_prompt(q, k, v, qi, ki, wi):
    B, T = q.shape[:2]
    k_top = min(TOPK_MAX, T // 4)
    nb = T // Q_BLOCK
    bidx = jnp.arange(B)[:, None, None]

    def blk(a):
        return a.reshape((B, nb, Q_BLOCK) + a.shape[2:]).swapaxes(0, 1)

    def one(inp):
        qb, qib, wib, posb = inp
        sc = index_scores(qib, wib, ki)
        idx, ok = indexer_topk(sc, posb, T, k_top)
        return sparse_attend(qb, k[bidx, idx], v[bidx, idx], ok)

    pos_blocks = jnp.arange(T, dtype=jnp.int32).reshape(nb, Q_BLOCK)
    o = lax.map(one, (blk(q), blk(qi), blk(wi), pos_blocks))
    return o.swapaxes(0, 1).reshape(B, T, ATT_W)


def dsa_sample(q, k, v, qi, ki, wi, cache_k, cache_v, cache_idx_k, page_table):
    DB, T = q.shape[:2]
    P = page_table.shape[1] * PAGE_SIZE
    L = P + T
    k_top = min(TOPK_MAX, L // 4)
    ki_past = cache_idx_k[page_table].reshape(DB, P, D_IDX)
    ki_all = jnp.concatenate([ki_past.astype(ki.dtype), ki], axis=1)
    q_pos = P + jnp.arange(T, dtype=jnp.int32)
    sc = index_scores(qi, wi, ki_all)
    idx, ok = indexer_topk(sc, q_pos, L, k_top)
    bidx = jnp.arange(DB)[:, None, None]
    pidx = jnp.minimum(idx, P - 1)
    phys = page_table[bidx, pidx // PAGE_SIZE]
    off = pidx % PAGE_SIZE
    nidx = jnp.clip(idx - P, 0, T - 1)
    is_new = (idx >= P)[..., None, None]
    k_sel = jnp.where(is_new, k[bidx, nidx], cache_k[phys, off].astype(k.dtype))
    v_sel = jnp.where(is_new, v[bidx, nidx], cache_v[phys, off].astype(v.dtype))
    return sparse_attend(q, k_sel, v_sel, ok).reshape(DB, T, ATT_W)


def moe(x, router_w, router_b, w_gate_up, b_gate_up, w_down, b_down):
    logits = (x @ router_w + router_b).astype(F32)
    top_v, top_i = lax.top_k(logits, TOP_K)
    gates = jax.nn.softmax(top_v, axis=-1)
    combine = jnp.einsum('nk,nke->ne', gates, jax.nn.one_hot(top_i, N_EXPERTS, dtype=F32))

    def expert(acc, inp):
        wgu, bgu, wd, bd, cw = inp
        h = (x @ wgu + bgu).astype(F32)
        gate = jnp.minimum(h[:, :D_FF], SWIGLU_LIMIT)
        up = jnp.clip(h[:, D_FF:], -SWIGLU_LIMIT, SWIGLU_LIMIT)
        act = ((up + 1.0) * gate * jax.nn.sigmoid(SWIGLU_ALPHA * gate)).astype(x.dtype)
        out = (act @ wd + bd).astype(F32)
        return acc + cw[:, None] * out, None

    acc, _ = lax.scan(expert, jnp.zeros(x.shape, F32), (w_gate_up, b_gate_up, w_down, b_down, combine.T))
    return acc.astype(x.dtype)


def run_layer(x, c, pos, attend, ret_s0, ada_w, ada_b, norm1_w, w_in, q_norm_w, k_norm_w,
              idx_k_norm_w, ret_gn_w, w_out, norm2_w, router_w, router_b,
              w_gate_up, b_gate_up, w_down, b_down):
    B, T, D = x.shape
    sh1, sc1, g1, sh2, sc2, g2 = jnp.split(jax.nn.silu(c) @ ada_w + ada_b, 6, axis=-1)
    h = rms_norm(x, norm1_w) * (1.0 + sc1[:, None]) + sh1[:, None]
    z = h @ w_in
    sizes = [RET_W] * 4 + [ATT_W] * 3 + [H_IDX * D_IDX, D_IDX, H_IDX]
    rq, rk, rv, rg, aq, ak, av, iq, ik, iw = jnp.split(z, np.cumsum(sizes)[:-1].tolist(), axis=-1)
    ret_out, ret_state = retention_group(rq, rk, rv, rg, pos, ret_s0, ret_gn_w)
    aq = rope(rms_norm(aq.reshape(B, T, H_ATT, D_HEAD), q_norm_w), pos)
    ak = rope(rms_norm(ak.reshape(B, T, H_ATT, D_HEAD), k_norm_w), pos)
    av = av.reshape(B, T, H_ATT, D_HEAD)
    iq = rope(iq.reshape(B, T, H_IDX, D_IDX), pos)
    ik = rope(rms_norm(ik, idx_k_norm_w), pos)
    iw = iw * (H_IDX ** -0.5)
    att_out = attend(aq, ak, av, iq, ik, iw)
    x = x + g1[:, None] * (jnp.concatenate([ret_out, att_out], axis=-1) @ w_out)
    h = rms_norm(x, norm2_w) * (1.0 + sc2[:, None]) + sh2[:, None]
    ff = moe(h.reshape(B * T, D), router_w, router_b, w_gate_up, b_gate_up, w_down, b_down).reshape(B, T, D)
    x = x + g2[:, None] * ff
    return x, ak, av, ik, ret_state


def setup_inputs(seed: int = 0) -> dict:
    key = jax.random.key(seed)
    ks = jax.random.split(key, 32)
    n_pages = PAST_LEN // PAGE_SIZE
    n_used = DEC_BATCH * n_pages
    n_pool = n_used + n_used // 4

    def nrm(k, shape, scale):
        return jax.random.normal(k, shape, F32) * scale

    page_table = jax.random.permutation(ks[0], n_pool)[:n_used].reshape(DEC_BATCH, n_pages).astype(jnp.int32)
    return {
        'x_prompt': nrm(ks[1], (BATCH, SEQ, D_MODEL), 1.0),
        'x_sample': nrm(ks[2], (DEC_BATCH, DEC_SEQ, D_MODEL), 1.0),
        'cache_k': nrm(ks[3], (DEPTH, n_pool, PAGE_SIZE, H_ATT, D_HEAD), 1.0),
        'cache_v': nrm(ks[4], (DEPTH, n_pool, PAGE_SIZE, H_ATT, D_HEAD), 1.0),
        'cache_idx_k': nrm(ks[5], (DEPTH, n_pool, PAGE_SIZE, D_IDX), 1.0),
        'state_ret': nrm(ks[6], (DEPTH, DEC_BATCH, H_RET, D_RET, D_RET), 1.0),
        'page_table': page_table,
        'c_prompt': nrm(ks[7], (BATCH, D_MODEL), 1.0),
        'c_sample': nrm(ks[8], (DEC_BATCH, D_MODEL), 1.0),
        'ada_w': nrm(ks[9], (DEPTH, D_MODEL, 6 * D_MODEL), 0.5 * D_MODEL ** -0.5),
        'ada_b': nrm(ks[10], (DEPTH, 6 * D_MODEL), 0.02),
        'norm1_w': 1.0 + nrm(ks[11], (DEPTH, D_MODEL), 0.02),
        'w_in': nrm(ks[12], (DEPTH, D_MODEL, IN_COLS), D_MODEL ** -0.5),
        'q_norm_w': 1.0 + nrm(ks[13], (DEPTH, D_HEAD), 0.02),
        'k_norm_w': 1.0 + nrm(ks[14], (DEPTH, D_HEAD), 0.02),
        'idx_k_norm_w': 1.0 + nrm(ks[15], (DEPTH, D_IDX), 0.02),
        'ret_gn_w': 1.0 + nrm(ks[16], (DEPTH, RET_W), 0.02),
        'w_out': nrm(ks[17], (DEPTH, MIX_WIDTH, D_MODEL), MIX_WIDTH ** -0.5),
        'norm2_w': 1.0 + nrm(ks[18], (DEPTH, D_MODEL), 0.02),
        'router_w': nrm(ks[19], (DEPTH, D_MODEL, N_EXPERTS), D_MODEL ** -0.5),
        'router_b': nrm(ks[20], (DEPTH, N_EXPERTS), 0.01),
        'w_gate_up': nrm(ks[21], (DEPTH, N_EXPERTS, D_MODEL, 2 * D_FF), D_MODEL ** -0.5),
        'b_gate_up': nrm(ks[22], (DEPTH, N_EXPERTS, 2 * D_FF), 0.01),
        'w_down': nrm(ks[23], (DEPTH, N_EXPERTS, D_FF, D_MODEL), D_FF ** -0.5),
        'b_down': nrm(ks[24], (DEPTH, N_EXPERTS, D_MODEL), 0.01),
    }


def reference(x_prompt, x_sample, cache_k, cache_v, cache_idx_k, state_ret, page_table,
              c_prompt, c_sample, ada_w, ada_b, norm1_w, w_in, q_norm_w, k_norm_w,
              idx_k_norm_w, ret_gn_w, w_out, norm2_w, router_w, router_b,
              w_gate_up, b_gate_up, w_down, b_down):
    B, T = x_prompt.shape[:2]
    DB, TS = x_sample.shape[:2]
    past = page_table.shape[1] * PAGE_SIZE
    pos_p = jnp.arange(T, dtype=jnp.int32)
    pos_s = past + jnp.arange(TS, dtype=jnp.int32)
    hp, hs = x_prompt, x_sample
    kp, vp, ikp, rp, ksl, vsl, iks, rs = [], [], [], [], [], [], [], []
    for l in range(DEPTH):
        lw = (ada_w[l], ada_b[l], norm1_w[l], w_in[l], q_norm_w[l], k_norm_w[l], idx_k_norm_w[l],
              ret_gn_w[l], w_out[l], norm2_w[l], router_w[l], router_b[l],
              w_gate_up[l], b_gate_up[l], w_down[l], b_down[l])
        s0 = jnp.zeros((B, H_RET, D_RET, D_RET), x_prompt.dtype)
        hp, a_k, a_v, i_k, r_s = run_layer(hp, c_prompt, pos_p, dsa_prompt, s0, *lw)
        kp.append(a_k); vp.append(a_v); ikp.append(i_k); rp.append(r_s)
        attend_s = functools.partial(dsa_sample, cache_k=cache_k[l], cache_v=cache_v[l],
                                     cache_idx_k=cache_idx_k[l], page_table=page_table)
        hs, a_k, a_v, i_k, r_s = run_layer(hs, c_sample, pos_s, attend_s, state_ret[l], *lw)
        ksl.append(a_k); vsl.append(a_v); iks.append(i_k); rs.append(r_s)
    k_prompt = jnp.stack(kp)
    v_prompt = jnp.stack(vp)
    idxk_prompt = jnp.stack(ikp)
    ret_prompt = jnp.stack(rp)
    k_sample = jnp.stack(ksl)
    v_sample = jnp.stack(vsl)
    idxk_sample = jnp.stack(iks)
    ret_sample = jnp.stack(rs)
    return (hp, hs, k_prompt, v_prompt, idxk_prompt, ret_prompt, k_sample, v_sample, idxk_sample, ret_sample)
```

```python
import functools
import math

import numpy as np
import jax
import jax.numpy as jnp
from jax import lax
from jax.experimental import pallas as pl
from jax.experimental.pallas import tpu as pltpu

F32 = jnp.float32
BF16 = jnp.bfloat16
I32 = jnp.int32

PAGE_SIZE = 128
H_IDX = 8
TOPK_MAX = 256
TOP_K = 4
SWIGLU_LIMIT = 7.0
SWIGLU_ALPHA = 1.702
ROPE_THETA = 10000.0
EPS = 1e-6
INT_MIN = -(2 ** 31)
NEG_INF = float("-inf")

LANES = 128
VMEM_LIMIT = 56 * 1024 * 1024


def _cparams(sem):
    return pltpu.CompilerParams(dimension_semantics=sem, vmem_limit_bytes=VMEM_LIMIT)


def _dot(a, b):
    return jnp.dot(a, b, preferred_element_type=F32)


def _dot_nt(a, b):
    return lax.dot_general(a, b, (((1,), (1,)), ((), ())), preferred_element_type=F32)


def _sigmoid(x):
    return 1.0 / (1.0 + jnp.exp(-x))


def _sortable_key(x):
    x = jnp.where(x == 0.0, 0.0, x)
    bits = pltpu.bitcast(x, I32)
    return jnp.where(bits < 0, bits ^ jnp.int32(0x7FFFFFFF), bits)


def _rope_halves(v, cos, sin_signed, first_half, width):
    swapped = jnp.where(first_half, pltpu.roll(v, width - 32, 1), pltpu.roll(v, 32, 1))
    return v * cos + swapped * sin_signed


def _inproj_body(x_ref, sc_ref, sh_ref, n1_ref, w_ref, wt_ref, cos_ref, sin_ref, qn_ref, kn_ref,
                 ikn_ref, gmat_ref,
                 rq_ref, rk_ref, rv_ref, rg_ref, aq_ref, ak_ref, akb_ref, av_ref, avb_ref, iq_ref,
                 tail_ref, *, seg_w, d_idx, iw_scale, k_scale, q_scale):
    x = x_ref[...]
    tm = x.shape[0]
    ms = jnp.mean(x * x, axis=-1, keepdims=True)
    xn = x * lax.rsqrt(ms + EPS) * n1_ref[...]
    h = (xn * (1.0 + sc_ref[...]) + sh_ref[...]).astype(BF16)

    cos1 = cos_ref[...]
    sin1 = sin_ref[...]
    reps = seg_w // LANES
    cos = jnp.concatenate([cos1] * reps, axis=1)
    sin = jnp.concatenate([sin1] * reps, axis=1)
    lane = lax.broadcasted_iota(I32, (tm, seg_w), 1)
    first = (lane & 32) == 0
    rope = functools.partial(_rope_halves, cos=cos, sin_signed=sin, first_half=first, width=seg_w)

    def seg(j):
        return _dot(h, w_ref[:, j * seg_w:(j + 1) * seg_w])

    def head_norm(v, w):
        sq = v * v
        hi = sq.astype(BF16)
        lo = (sq - hi.astype(F32)).astype(BF16)
        msq = _dot(hi, gmat_ref[...]) + _dot(lo, gmat_ref[...])
        return v * lax.rsqrt(msq + EPS) * w

    rq_ref[...] = rope(seg(0)).astype(rq_ref.dtype)
    rk_ref[...] = (rope(seg(1)) * k_scale).astype(rk_ref.dtype)
    rv_ref[...] = seg(2).astype(rv_ref.dtype)
    rg_ref[...] = seg(3).astype(rg_ref.dtype)
    aq_ref[...] = (rope(head_norm(seg(4), qn_ref[...])) * q_scale).astype(aq_ref.dtype)
    ak = rope(head_norm(seg(5), kn_ref[...]))
    ak_ref[...] = ak
    akb_ref[...] = ak.astype(BF16)
    av = seg(6)
    av_ref[...] = av
    avb_ref[...] = av.astype(BF16)
    iq_ref[...] = rope(seg(7)).astype(iq_ref.dtype)

    zt = _dot(h, wt_ref[...])
    lane_t = lax.broadcasted_iota(I32, (tm, LANES), 1)
    is_key = lane_t < d_idx
    msk = jnp.sum(jnp.where(is_key, zt * zt, 0.0), axis=-1, keepdims=True) * (1.0 / d_idx)
    kn = zt * lax.rsqrt(msk + EPS) * ikn_ref[...]
    kr = _rope_halves(kn, cos1, sin1, (lane_t & 32) == 0, LANES)
    tail_ref[...] = jnp.where(is_key, kr, zt * iw_scale)


def _inproj(x2d, sc, sh, n1, w_main, w_tail, cos, sin, qn, kn, ikn, gmat, *, tm, rows_per_batch,
            per_row, ret_dtype, d_ret, d_head, d_idx):
    n, d = x2d.shape
    seg_w = w_main.shape[1] // 8
    grid = (n // tm,)
    row = lambda i: (i, 0)
    const = lambda i: (0, 0)
    if per_row:
        mod_spec = pl.BlockSpec((tm, d), row)
        pos_spec = pl.BlockSpec((tm, LANES), row)
    else:
        tiles_per_batch = rows_per_batch // tm
        mod_spec = pl.BlockSpec((None, 1, d), lambda i: (i // tiles_per_batch, 0, 0))
        pos_spec = pl.BlockSpec((tm, LANES), lambda i: (i % tiles_per_batch, 0))
    in_specs = [
        pl.BlockSpec((tm, d), row), mod_spec, mod_spec, pl.BlockSpec((1, d), const),
        pl.BlockSpec(w_main.shape, const), pl.BlockSpec(w_tail.shape, const),
        pos_spec, pos_spec,
        pl.BlockSpec((1, seg_w), const), pl.BlockSpec((1, seg_w), const), pl.BlockSpec((1, LANES), const),
        pl.BlockSpec(gmat.shape, const),
    ]
    seg_spec = pl.BlockSpec((tm, seg_w), row)
    out_dtypes = [ret_dtype, ret_dtype, ret_dtype, ret_dtype, BF16, F32, BF16, F32, BF16, BF16]
    out_shape = [jax.ShapeDtypeStruct((n, seg_w), dt) for dt in out_dtypes]
    out_shape.append(jax.ShapeDtypeStruct((n, LANES), F32))
    out_specs = [seg_spec] * 10 + [pl.BlockSpec((tm, LANES), row)]
    body = functools.partial(_inproj_body, seg_w=seg_w, d_idx=d_idx, iw_scale=H_IDX ** -0.5,
                             k_scale=d_ret ** -0.5, q_scale=d_head ** -0.5)
    return pl.pallas_call(
        body, grid=grid, in_specs=in_specs, out_specs=out_specs, out_shape=out_shape,
        compiler_params=_cparams(("arbitrary",)), name="inproj",
    )(x2d, sc, sh, n1, w_main, w_tail, cos, sin, qn, kn, ikn, gmat)


def _group_norm_gate(o, g, gnw):
    mu = jnp.mean(o, axis=-1, keepdims=True)
    var = jnp.mean((o - mu) ** 2, axis=-1, keepdims=True)
    on = (o - mu) * lax.rsqrt(var + EPS) * gnw
    g = g.astype(F32)
    return g * _sigmoid(g) * on


def _ret_prompt_body(lg_ref, q_ref, kt_ref, v_ref, g_ref, gnw_ref, o_ref, s_ref, *, chunk):
    t, dk = q_ref.shape
    lg = lg_ref[pl.program_id(1)]
    ii = lax.broadcasted_iota(I32, (chunk, chunk), 0)
    jj = lax.broadcasted_iota(I32, (chunk, chunk), 1)
    rel = (ii - jj).astype(F32)
    decay = jnp.where(rel >= 0.0, jnp.exp(jnp.maximum(rel, 0.0) * lg), 0.0)
    q_dec = jnp.exp((lax.broadcasted_iota(I32, (chunk, 1), 0).astype(F32) + 1.0) * lg)
    k_dec = jnp.exp((chunk - 1.0 - lax.broadcasted_iota(I32, (1, chunk), 1).astype(F32)) * lg)
    g_chunk = jnp.exp(jnp.full((1, 1), chunk, F32) * lg)
    gnw = gnw_ref[...]
    state = jnp.zeros((dk, v_ref.shape[1]), F32)
    for c in range(t // chunk):
        rows = slice(c * chunk, (c + 1) * chunk)
        qc = q_ref[rows, :]
        ktc = kt_ref[:, rows]
        vc = v_ref[rows, :]
        inner = _dot(qc, ktc) * decay
        o = _dot(inner.astype(BF16), vc) + _dot(qc, state.astype(BF16)) * q_dec
        state = state * g_chunk + _dot((ktc.astype(F32) * k_dec).astype(BF16), vc)
        o_ref[rows, :] = _group_norm_gate(o, g_ref[rows, :], gnw).astype(o_ref.dtype)
    s_ref[...] = state


def _ret_prompt(log_g, q, kt, v, g, gnw, *, chunk):
    b, h, t, dk = q.shape
    dv = v.shape[-1]
    blk = lambda r, c: pl.BlockSpec((None, None, r, c), lambda i, j, lg: (i, j, 0, 0))
    grid_spec = pltpu.PrefetchScalarGridSpec(
        num_scalar_prefetch=1, grid=(b, h),
        in_specs=[blk(t, dk), blk(dk, t), blk(t, dv), blk(t, dv),
                  pl.BlockSpec((None, 1, dv), lambda i, j, lg: (j, 0, 0))],
        out_specs=[blk(t, dv), blk(dk, dv)],
    )
    return pl.pallas_call(
        functools.partial(_ret_prompt_body, chunk=chunk), grid_spec=grid_spec,
        out_shape=[jax.ShapeDtypeStruct((b, h, t, dv), BF16), jax.ShapeDtypeStruct((b, h, dk, dv), F32)],
        compiler_params=_cparams(("arbitrary", "arbitrary")), name="ret_prompt",
    )(log_g, q, kt, v, g, gnw)


def _ret_sample_body(lg_ref, qc_ref, kc_ref, v_ref, g_ref, s0_ref, gnw_ref, o_ref, s_ref):
    for h in range(qc_ref.shape[0]):
        gamma = jnp.exp(jnp.full((1, 1), 1.0, F32) * lg_ref[h])
        qc = qc_ref[h]
        kc = kc_ref[h]
        v = v_ref[h]
        s0 = s0_ref[h]
        qk = jnp.sum(qc * kc, axis=0, keepdims=True)
        o = qk * v + jnp.sum(qc * s0, axis=0, keepdims=True) * gamma
        s_ref[h] = s0 * gamma + kc * v
        o_ref[h] = _group_norm_gate(o, g_ref[h], gnw_ref[h]).astype(o_ref.dtype)


def _ret_sample(log_g, q, k, v, g, s0, gnw):
    db, h, dk = q.shape
    dv = v.shape[-1]
    col = pl.BlockSpec((None, h, dk, 1), lambda i, lg: (i, 0, 0, 0))
    rowv = pl.BlockSpec((None, h, 1, dv), lambda i, lg: (i, 0, 0, 0))
    mat = pl.BlockSpec((None, h, dk, dv), lambda i, lg: (i, 0, 0, 0))
    grid_spec = pltpu.PrefetchScalarGridSpec(
        num_scalar_prefetch=1, grid=(db,),
        in_specs=[col, col, rowv, rowv, mat, pl.BlockSpec((h, 1, dv), lambda i, lg: (0, 0, 0))],
        out_specs=[rowv, mat],
    )
    o, s = pl.pallas_call(
        _ret_sample_body, grid_spec=grid_spec,
        out_shape=[jax.ShapeDtypeStruct((db, h, 1, dv), BF16), jax.ShapeDtypeStruct((db, h, dk, dv), F32)],
        compiler_params=_cparams(("arbitrary",)), name="ret_sample",
    )(log_g, q.reshape(db, h, dk, 1), k.reshape(db, h, dk, 1), v.reshape(db, h, 1, dv),
      g.reshape(db, h, 1, dv), s0, gnw)
    return o.reshape(db, h * dv), s


def _select_topk(key_ref, pos, visible, k_top, n_pos_bits, axis):
    def count(mask):
        return jnp.sum(jnp.where(mask, 1.0, 0.0), axis=axis, keepdims=True)

    kf = float(k_top)
    c0 = count(key_ref[...] >= 0)
    cur = jnp.where(c0 >= kf, jnp.int32(0), jnp.int32(INT_MIN))

    def value_step(i, cur):
        cand = cur + (jnp.int32(1) << (30 - i))
        return jnp.where(count(key_ref[...] >= cand) >= kf, cand, cur)

    thr = lax.fori_loop(0, 31, value_step, cur)
    need = kf - count(key_ref[...] > thr)

    def tie_step(i, m):
        cand = m + (jnp.int32(1) << (n_pos_bits - 1 - i))
        below = count((key_ref[...] == thr) & (pos < cand))
        return jnp.where(below < need, cand, m)

    cut = lax.fori_loop(0, n_pos_bits, tie_step, jnp.zeros_like(thr))
    key = key_ref[...]
    return visible & ((key > thr) | ((key == thr) & (pos <= cut)))


def _dsa_prompt_body(ik_ref, iqt_ref, iwt_ref, aqt_ref, k_ref, vt_ref, ot_ref, key_ref, bias_ref,
                     *, k_top, d_idx, d_head):
    t, qb = key_ref.shape
    j = pl.program_id(1)
    krow = lax.broadcasted_iota(I32, (t, qb), 0)
    qpos = j * qb + lax.broadcasted_iota(I32, (t, qb), 1)
    visible = krow <= qpos

    ik = ik_ref[...]
    sc = jnp.zeros((t, qb), F32)
    for h in range(iwt_ref.shape[0]):
        s = _dot(ik, iqt_ref[h * d_idx:(h + 1) * d_idx, :])
        sc = sc + jnp.maximum(s, 0.0) * iwt_ref[h:h + 1, :]
    key_ref[...] = jnp.where(visible, _sortable_key(sc), jnp.int32(INT_MIN))
    sel = _select_topk(key_ref, krow, visible, k_top, int(math.log2(t)), 0)
    bias_ref[...] = jnp.where(sel, 0.0, NEG_INF)

    for h in range(k_ref.shape[0]):
        rows = slice(h * d_head, (h + 1) * d_head)
        s = _dot(k_ref[h], aqt_ref[rows, :]) + bias_ref[...]
        m = jnp.max(s, axis=0, keepdims=True)
        p = jnp.exp(s - m)
        l = jnp.sum(p, axis=0, keepdims=True)
        o = _dot(vt_ref[rows, :], p.astype(BF16))
        ot_ref[rows, :] = (o / l).astype(ot_ref.dtype)


def _dsa_prompt(ik, iqt, iwt, aqt, kh, vt, *, qb, k_top):
    b, t, d_idx = ik.shape
    n_heads, d_head = kh.shape[1], kh.shape[3]
    w = vt.shape[1]
    per_b = lambda r, c: pl.BlockSpec((None, r, c), lambda i, j: (i, 0, 0))
    per_q = lambda r: pl.BlockSpec((None, r, qb), lambda i, j: (i, 0, j))
    return pl.pallas_call(
        functools.partial(_dsa_prompt_body, k_top=k_top, d_idx=d_idx, d_head=d_head),
        grid=(b, t // qb),
        in_specs=[per_b(t, d_idx), per_q(iqt.shape[1]), per_q(iwt.shape[1]), per_q(w),
                  pl.BlockSpec((None, n_heads, t, d_head), lambda i, j: (i, 0, 0, 0)), per_b(w, t)],
        out_specs=per_q(w),
        out_shape=jax.ShapeDtypeStruct((b, w, t), BF16),
        scratch_shapes=[pltpu.VMEM((t, qb), I32), pltpu.VMEM((t, qb), F32)],
        compiler_params=_cparams(("arbitrary", "arbitrary")), name="dsa_prompt",
    )(ik, iqt, iwt, aqt, kh, vt)


def _page_copies(cache_ref, buf_ref, sem_ref, pt_ref, seq, first_page, n_pages, slot):
    return [pltpu.make_async_copy(cache_ref.at[pt_ref[seq, first_page + p]],
                                  buf_ref.at[slot, pl.ds(p * PAGE_SIZE, PAGE_SIZE)],
                                  sem_ref.at[slot])
            for p in range(n_pages)]


def _sidx_body(pt_ref, qi_ref, iw_ref, iknew_ref, cache_ref, out_ref, buf_ref, sem_ref, *, n_pages):
    b = pl.program_id(0)
    nb = pl.num_programs(0)
    slot = b % 2
    past = n_pages * PAGE_SIZE

    def start(seq, s):
        for c in _page_copies(cache_ref, buf_ref, sem_ref, pt_ref, seq, 0, n_pages, s):
            c.start()

    @pl.when(b == 0)
    def _():
        buf_ref[:, past:, :] = jnp.zeros((2, LANES, buf_ref.shape[2]), F32)
        start(0, 0)

    @pl.when(b + 1 < nb)
    def _():
        start(b + 1, 1 - slot)

    for c in _page_copies(cache_ref, buf_ref, sem_ref, pt_ref, b, 0, n_pages, slot):
        c.wait()
    buf_ref[slot, past:past + 1, :] = iknew_ref[...]

    keys = buf_ref[slot].astype(BF16)
    s = _dot_nt(qi_ref[...], keys)
    sc = jnp.sum(jnp.maximum(s, 0.0) * iw_ref[...], axis=0, keepdims=True)
    out_ref[...] = sc


def _sample_index_scores(page_table, qi, iw, ik_new, cache_idx_k):
    db, n_pages = page_table.shape
    d_idx = cache_idx_k.shape[-1]
    n_keys = n_pages * PAGE_SIZE + LANES
    grid_spec = pltpu.PrefetchScalarGridSpec(
        num_scalar_prefetch=1, grid=(db,),
        in_specs=[pl.BlockSpec((None, H_IDX, d_idx), lambda i, pt: (i, 0, 0)),
                  pl.BlockSpec((None, H_IDX, 1), lambda i, pt: (i, 0, 0)),
                  pl.BlockSpec((None, 1, d_idx), lambda i, pt: (i, 0, 0)),
                  pl.BlockSpec(memory_space=pl.ANY)],
        out_specs=pl.BlockSpec((None, 1, n_keys), lambda i, pt: (i, 0, 0)),
        scratch_shapes=[pltpu.VMEM((2, n_keys, d_idx), F32), pltpu.SemaphoreType.DMA((2,))],
    )
    out = pl.pallas_call(
        functools.partial(_sidx_body, n_pages=n_pages), grid_spec=grid_spec,
        out_shape=jax.ShapeDtypeStruct((db, 1, n_keys), F32),
        compiler_params=_cparams(("arbitrary",)), name="sample_index_scores",
    )(page_table, qi, iw, ik_new, cache_idx_k)
    return out.reshape(db, n_keys)


def _ssel_body(sc_ref, sel_ref, key_ref, *, n_visible, k_top):
    n, w = sc_ref.shape
    pos = lax.broadcasted_iota(I32, (n, w), 1)
    visible = pos < n_visible
    key_ref[...] = jnp.where(visible, _sortable_key(sc_ref[...]), jnp.int32(INT_MIN))
    sel = _select_topk(key_ref, pos, visible, k_top, int(math.ceil(math.log2(w))), 1)
    sel_ref[...] = jnp.where(sel, 1.0, 0.0)


def _sample_select(scores, *, n_visible, k_top):
    return pl.pallas_call(
        functools.partial(_ssel_body, n_visible=n_visible, k_top=k_top),
        out_shape=jax.ShapeDtypeStruct(scores.shape, F32),
        scratch_shapes=[pltpu.VMEM(scores.shape, I32)],
        compiler_params=pltpu.CompilerParams(vmem_limit_bytes=VMEM_LIMIT), name="sample_select",
    )(scores)


def _sattn_body(pt_ref, q_ref, sel_ref, selnew_ref, knew_ref, vnew_ref, ck_ref, cv_ref, o_ref,
                kbuf_ref, vbuf_ref, ksem_ref, vsem_ref, m_ref, l_ref, acc_ref, *, group, d_head):
    b = pl.program_id(0)
    g = pl.program_id(1)
    ng = pl.num_programs(1)
    step = b * ng + g
    total = pl.num_programs(0) * ng
    slot = step % 2
    n_heads, w = acc_ref.shape

    def copies(seq, grp, s):
        return (_page_copies(ck_ref, kbuf_ref, ksem_ref, pt_ref, seq, grp * group, group, s)
                + _page_copies(cv_ref, vbuf_ref, vsem_ref, pt_ref, seq, grp * group, group, s))

    @pl.when(step == 0)
    def _():
        for c in copies(0, 0, 0):
            c.start()

    @pl.when(step + 1 < total)
    def _():
        nxt = step + 1
        for c in copies(nxt // ng, nxt % ng, 1 - slot):
            c.start()

    for c in copies(b, g, slot):
        c.wait()

    @pl.when(g == 0)
    def _():
        m_ref[...] = jnp.full(m_ref.shape, NEG_INF, F32)
        l_ref[...] = jnp.zeros(l_ref.shape, F32)
        acc_ref[...] = jnp.zeros(acc_ref.shape, F32)

    head_of_lane = lax.broadcasted_iota(I32, (n_heads, w), 1) // d_head
    own = head_of_lane == lax.broadcasted_iota(I32, (n_heads, w), 0)
    q = q_ref[...]
    qbd = jnp.where(own, q, 0.0)

    def online_update(s, pv_fn):
        m_old = m_ref[...]
        m_new = jnp.maximum(m_old, jnp.max(s, axis=-1, keepdims=True))
        m_safe = jnp.where(m_new == NEG_INF, 0.0, m_new)
        alpha = jnp.exp(m_old - m_safe)
        p = jnp.exp(s - m_safe)
        l_ref[...] = alpha * l_ref[...] + jnp.sum(p, axis=-1, keepdims=True)
        acc_ref[...] = alpha * acc_ref[...] + pv_fn(p)
        m_ref[...] = m_new

    kb = kbuf_ref[slot].astype(BF16)
    s = _dot_nt(qbd.astype(BF16), kb) + jnp.where(sel_ref[...] > 0.0, 0.0, NEG_INF)
    online_update(s, lambda p: _dot(p.astype(BF16), vbuf_ref[slot].astype(BF16)))

    @pl.when(g == ng - 1)
    def _():
        s_new = jnp.sum(qbd * knew_ref[...], axis=-1, keepdims=True)
        s_new = s_new + jnp.where(selnew_ref[:, 0:1] > 0.0, 0.0, NEG_INF)
        online_update(s_new, lambda p: p * vnew_ref[...])
        o = acc_ref[...] / l_ref[...]
        o_ref[...] = jnp.sum(jnp.where(own, o, 0.0), axis=0, keepdims=True)


def _sample_attention(page_table, q, sel_past, sel_new, k_new, v_new, cache_k, cache_v, *, group, d_head):
    db, n_pages = page_table.shape
    w = q.shape[-1]
    ng = n_pages // group
    gk = group * PAGE_SIZE
    n_heads = w // d_head
    row = lambda c: pl.BlockSpec((None, 1, c), lambda i, j, pt: (i, 0, 0))
    grid_spec = pltpu.PrefetchScalarGridSpec(
        num_scalar_prefetch=1, grid=(db, ng),
        in_specs=[row(w),
                  pl.BlockSpec((None, None, 1, gk), lambda i, j, pt: (i, j, 0, 0)),
                  row(LANES), row(w), row(w),
                  pl.BlockSpec(memory_space=pl.ANY), pl.BlockSpec(memory_space=pl.ANY)],
        out_specs=row(w),
        scratch_shapes=[pltpu.VMEM((2, gk, w), F32), pltpu.VMEM((2, gk, w), F32),
                        pltpu.SemaphoreType.DMA((2,)), pltpu.SemaphoreType.DMA((2,)),
                        pltpu.VMEM((n_heads, 1), F32), pltpu.VMEM((n_heads, 1), F32),
                        pltpu.VMEM((n_heads, w), F32)],
    )
    out = pl.pallas_call(
        functools.partial(_sattn_body, group=group, d_head=d_head), grid_spec=grid_spec,
        out_shape=jax.ShapeDtypeStruct((db, 1, w), F32),
        compiler_params=_cparams(("arbitrary", "arbitrary")), name="sample_attention",
    )(page_table, q.reshape(db, 1, w), sel_past.reshape(db, ng, 1, gk), sel_new.reshape(db, 1, LANES),
      k_new.reshape(db, 1, w), v_new.reshape(db, 1, w), cache_k, cache_v)
    return out.reshape(db, w)


def _outproj_body(x_ref, ret_ref, att_ref, wo_ref, g1_ref, sc2_ref, sh2_ref, n2_ref, rwh_ref, rwl_ref,
                  rb_ref, x1_ref, h2_ref, cw_ref, *, n_experts):
    half = ret_ref.shape[1]
    y = _dot(ret_ref[...], wo_ref[:half, :]) + _dot(att_ref[...], wo_ref[half:, :])
    x1 = x_ref[...] + g1_ref[...] * y
    x1_ref[...] = x1
    ms = jnp.mean(x1 * x1, axis=-1, keepdims=True)
    h2 = x1 * lax.rsqrt(ms + EPS) * n2_ref[...] * (1.0 + sc2_ref[...]) + sh2_ref[...]
    hi = h2.astype(BF16)
    h2_ref[...] = hi
    lo = (h2 - hi.astype(F32)).astype(BF16)
    logits = _dot(hi, rwh_ref[...]) + _dot(lo, rwh_ref[...]) + _dot(hi, rwl_ref[...]) + rb_ref[...]

    tm = logits.shape[0]
    lane = lax.broadcasted_iota(I32, (tm, LANES), 1).astype(F32)
    work = jnp.where(lane < n_experts, logits, NEG_INF)
    top_vals, top_hot = [], []
    for _ in range(TOP_K):
        m = jnp.max(work, axis=-1, keepdims=True)
        first = jnp.min(jnp.where(work == m, lane, float(LANES)), axis=-1, keepdims=True)
        hot = lane == first
        top_vals.append(m)
        top_hot.append(hot)
        work = jnp.where(hot, NEG_INF, work)
    exps = [jnp.exp(v - top_vals[0]) for v in top_vals]
    den = exps[0]
    for e in exps[1:]:
        den = den + e
    cw = jnp.zeros((tm, LANES), F32)
    for e, hot in zip(exps, top_hot):
        cw = cw + jnp.where(hot, e / den, 0.0)
    cw_ref[...] = cw


def _outproj(x2d, ret, att, wo, g1, sc2, sh2, n2, rwh, rwl, rb, *, tm, rows_per_batch, per_row, n_experts):
    n, d = x2d.shape
    half = ret.shape[1]
    row = lambda i: (i, 0)
    const = lambda i: (0, 0)
    if per_row:
        mod_spec = pl.BlockSpec((tm, d), row)
    else:
        tiles_per_batch = rows_per_batch // tm
        mod_spec = pl.BlockSpec((None, 1, d), lambda i: (i // tiles_per_batch, 0, 0))
    return pl.pallas_call(
        functools.partial(_outproj_body, n_experts=n_experts), grid=(n // tm,),
        in_specs=[pl.BlockSpec((tm, d), row), pl.BlockSpec((tm, half), row), pl.BlockSpec((tm, half), row),
                  pl.BlockSpec(wo.shape, const), mod_spec, mod_spec, mod_spec, pl.BlockSpec((1, d), const),
                  pl.BlockSpec(rwh.shape, const), pl.BlockSpec(rwl.shape, const), pl.BlockSpec((1, LANES), const)],
        out_specs=[pl.BlockSpec((tm, d), row), pl.BlockSpec((tm, d), row), pl.BlockSpec((tm, LANES), row)],
        out_shape=[jax.ShapeDtypeStruct((n, d), F32), jax.ShapeDtypeStruct((n, d), BF16),
                   jax.ShapeDtypeStruct((n, LANES), F32)],
        compiler_params=_cparams(("arbitrary",)), name="outproj",
    )(x2d, ret, att, wo, g1, sc2, sh2, n2, rwh, rwl, rb)


def _moe_body(h_ref, cw_ref, x1_ref, g2_ref, wgu_ref, bgu_ref, wd_ref, bd_ref, out_ref, acc_ref, *, f_chunk):
    e = pl.program_id(1)
    tm = h_ref.shape[0]
    d_ff = wd_ref.shape[0]

    @pl.when(e == 0)
    def _():
        acc_ref[...] = jnp.zeros(acc_ref.shape, F32)

    lane = lax.broadcasted_iota(I32, (tm, LANES), 1)
    c = jnp.sum(jnp.where(lane == e, cw_ref[...], 0.0), axis=-1, keepdims=True)
    h = h_ref[...]
    y = jnp.zeros(acc_ref.shape, F32)
    for j in range(d_ff // f_chunk):
        cols = slice(j * f_chunk, (j + 1) * f_chunk)
        ucols = slice(d_ff + j * f_chunk, d_ff + (j + 1) * f_chunk)
        hg = _dot(h, wgu_ref[:, cols]) + bgu_ref[:, cols]
        hu = _dot(h, wgu_ref[:, ucols]) + bgu_ref[:, ucols]
        gate = jnp.minimum(hg, SWIGLU_LIMIT)
        up = jnp.clip(hu, -SWIGLU_LIMIT, SWIGLU_LIMIT)
        act = (up + 1.0) * gate * _sigmoid(SWIGLU_ALPHA * gate)
        y = y + _dot(act.astype(BF16), wd_ref[cols, :])
    acc_ref[...] += c * (y + bd_ref[...])

    @pl.when(e == pl.num_programs(1) - 1)
    def _():
        out_ref[...] = x1_ref[...] + g2_ref[...] * acc_ref[...]


def _moe(h2, cw, x1, g2, wgu, bgu, wd, bd, *, tm, rows_per_batch, per_row, f_chunk):
    n, d = h2.shape
    n_exp, _, two_f = wgu.shape
    d_ff = two_f // 2
    row = lambda i, e: (i, 0)
    if per_row:
        mod_spec = pl.BlockSpec((tm, d), row)
    else:
        tiles_per_batch = rows_per_batch // tm
        mod_spec = pl.BlockSpec((None, 1, d), lambda i, e: (i // tiles_per_batch, 0, 0))
    return pl.pallas_call(
        functools.partial(_moe_body, f_chunk=f_chunk), grid=(n // tm, n_exp),
        in_specs=[pl.BlockSpec((tm, d), row), pl.BlockSpec((tm, LANES), row), pl.BlockSpec((tm, d), row), mod_spec,
                  pl.BlockSpec((None, d, two_f), lambda i, e: (e, 0, 0)),
                  pl.BlockSpec((None, 1, two_f), lambda i, e: (e, 0, 0)),
                  pl.BlockSpec((None, d_ff, d), lambda i, e: (e, 0, 0)),
                  pl.BlockSpec((None, 1, d), lambda i, e: (e, 0, 0))],
        out_specs=pl.BlockSpec((tm, d), row),
        out_shape=jax.ShapeDtypeStruct((n, d), F32),
        scratch_shapes=[pltpu.VMEM((tm, d), F32)],
        compiler_params=_cparams(("arbitrary", "arbitrary")), name="moe",
    )(h2, cw, x1, g2, wgu, bgu, wd, bd)


def _rope_tables(pos, d):
    inv = jnp.power(ROPE_THETA, -jnp.arange(0, d, 2, dtype=F32) / d)
    ang = pos.astype(F32)[:, None] * inv[None, :]
    cos = jnp.cos(jnp.concatenate([ang, ang], axis=-1))
    sin = jnp.sin(ang)
    sin = jnp.concatenate([-sin, sin], axis=-1)
    reps = LANES // d
    return jnp.tile(cos, (1, reps)), jnp.tile(sin, (1, reps))


def _split_bf16(x):
    hi = x.astype(BF16)
    return hi, (x - hi.astype(F32)).astype(BF16)


def _adaln_body(c_ref, w_ref, b_ref, o_ref):
    c = c_ref[...]
    a_hi, a_lo = _split_bf16(c * _sigmoid(c))
    w_hi, w_lo = _split_bf16(w_ref[...])
    o_ref[...] = _dot(a_hi, w_hi) + _dot(a_lo, w_hi) + _dot(a_hi, w_lo) + b_ref[...]


def _adaln(c, ada_w, ada_b):
    n, d = c.shape
    n_out = ada_w.shape[1]
    tn = d
    out = pl.pallas_call(
        _adaln_body, grid=(n_out // tn,),
        in_specs=[pl.BlockSpec((n, d), lambda j: (0, 0)), pl.BlockSpec((d, tn), lambda j: (0, j)),
                  pl.BlockSpec((1, tn), lambda j: (0, j))],
        out_specs=pl.BlockSpec((n, tn), lambda j: (0, j)),
        out_shape=jax.ShapeDtypeStruct((n, n_out), F32),
        compiler_params=_cparams(("arbitrary",)), name="adaln",
    )(c, ada_w, ada_b[None])
    return jnp.split(out, 6, axis=-1)


def kernel(x_prompt, x_sample, cache_k, cache_v, cache_idx_k, state_ret, page_table, c_prompt, c_sample,
           ada_w, ada_b, norm1_w, w_in, q_norm_w, k_norm_w, idx_k_norm_w, ret_gn_w, w_out, norm2_w,
           router_w, router_b, w_gate_up, b_gate_up, w_down, b_down):
    depth = w_in.shape[0]
    b, t, d = x_prompt.shape
    db, ts, _ = x_sample.shape
    assert ts == 1, "the sample group decodes one token per sequence"
    _, _, h_ret, d_ret, _ = state_ret.shape
    _, n_pool, page, h_att, d_head = cache_k.shape
    assert page == PAGE_SIZE
    d_idx = cache_idx_k.shape[-1]
    ret_w, att_w = h_ret * d_ret, h_att * d_head
    assert ret_w == att_w == H_IDX * d_idx and d_ret == d_head == d_idx == 64
    n_experts = router_w.shape[-1]
    n_pages = page_table.shape[1]
    past = n_pages * PAGE_SIZE
    n_main = 8 * ret_w

    cos_p, sin_p = _rope_tables(jnp.arange(t, dtype=jnp.int32), d_ret)
    cos_s, sin_s = _rope_tables(jnp.full((db,), past, jnp.int32), d_ret)
    log_g = jnp.log1p(-jnp.power(2.0, -5.0 - jnp.arange(h_ret, dtype=F32)))
    gmat = jnp.asarray(np.kron(np.eye(ret_w // d_head), np.full((d_head, d_head), 1.0 / d_head)), BF16)

    hp, hs = x_prompt.reshape(b * t, d), x_sample.reshape(db, d)
    outs = [[] for _ in range(8)]
    for l in range(depth):
        w_main = w_in[l, :, :n_main].astype(BF16)
        w_tail = jnp.pad(w_in[l, :, n_main:], ((0, 0), (0, LANES - (w_in.shape[2] - n_main)))).astype(BF16)
        qn = jnp.tile(q_norm_w[l], h_att)[None]
        kn = jnp.tile(k_norm_w[l], h_att)[None]
        ikn = jnp.pad(idx_k_norm_w[l], (0, LANES - d_idx))[None]
        gnw = ret_gn_w[l].reshape(h_ret, 1, d_ret)
        wo = w_out[l].astype(BF16)
        rw = jnp.pad(router_w[l], ((0, 0), (0, LANES - n_experts)))
        rwh = rw.astype(BF16)
        rwl = (rw - rwh.astype(F32)).astype(BF16)
        rb = jnp.pad(router_b[l], (0, LANES - n_experts))[None]
        n1, n2 = norm1_w[l][None], norm2_w[l][None]
        wgu, wd = w_gate_up[l].astype(BF16), w_down[l].astype(BF16)
        bgu, bd = b_gate_up[l][:, None, :], b_down[l][:, None, :]
        inproj = functools.partial(_inproj, n1=n1, w_main=w_main, w_tail=w_tail, qn=qn, kn=kn, ikn=ikn, gmat=gmat,
                                   d_ret=d_ret, d_head=d_head, d_idx=d_idx)
        outproj = functools.partial(_outproj, wo=wo, n2=n2, rwh=rwh, rwl=rwl, rb=rb, n_experts=n_experts)
        moe = functools.partial(_moe, wgu=wgu, bgu=bgu, wd=wd, bd=bd, f_chunk=512)

        mods = _adaln(jnp.concatenate([c_prompt, c_sample], axis=0), ada_w[l], ada_b[l])

        sh1, sc1, g1, sh2, sc2, g2 = [a[:b, None, :] for a in mods]
        rq, rk, rv, rg, aq, ak, akb, av, avb, iq, tail = inproj(
            hp, sc1, sh1, cos=cos_p, sin=sin_p, tm=256, rows_per_batch=t, per_row=False, ret_dtype=BF16)

        heads = lambda a, nh: a.reshape(b, t, nh, -1).transpose(0, 2, 1, 3)
        ret_o, ret_s = _ret_prompt(log_g, heads(rq, h_ret), heads(rk, h_ret).swapaxes(2, 3), heads(rv, h_ret),
                                   heads(rg, h_ret), gnw, chunk=256)
        ret_o = ret_o.transpose(0, 2, 1, 3).reshape(b * t, ret_w)

        tail3 = tail.reshape(b, t, LANES)
        ikp = tail3[:, :, :d_idx]
        seq_t = lambda a: a.reshape(b, t, -1).swapaxes(1, 2)
        iwt = tail3[:, :, d_idx:d_idx + H_IDX].swapaxes(1, 2) * (d_idx ** -0.5)
        att_t = _dsa_prompt(ikp.astype(BF16), seq_t(iq), iwt, seq_t(aq), heads(akb, h_att), seq_t(avb),
                            qb=128, k_top=min(TOPK_MAX, t // 4))
        att_o = att_t.swapaxes(1, 2).reshape(b * t, att_w)

        x1, h2, cw = outproj(hp, ret_o, att_o, g1=g1, sc2=sc2, sh2=sh2, tm=512, rows_per_batch=t, per_row=False)
        hp = moe(h2, cw, x1, g2, tm=min(1024, t), rows_per_batch=t, per_row=False)
        outs[0].append(ak.reshape(b, t, h_att, d_head))
        outs[1].append(av.reshape(b, t, h_att, d_head))
        outs[2].append(ikp)
        outs[3].append(ret_s)

        sh1, sc1, g1, sh2, sc2, g2 = [a[b:] for a in mods]
        rq, rk, rv, rg, aq, ak, akb, av, avb, iq, tail = inproj(
            hs, sc1, sh1, cos=cos_s, sin=sin_s, tm=db, rows_per_batch=1, per_row=True, ret_dtype=F32)
        hd = lambda a: a.reshape(db, h_ret, d_ret)
        ret_o, ret_s = _ret_sample(log_g, hd(rq), hd(rk), hd(rv), hd(rg), state_ret[l], gnw)

        iks = tail[:, :d_idx]
        iw = tail[:, d_idx:d_idx + H_IDX] * (d_idx ** -0.5)
        scores = _sample_index_scores(page_table, iq.reshape(db, H_IDX, d_idx), iw[:, :, None], iks[:, None, :],
                                      cache_idx_k[l])
        sel = _sample_select(scores, n_visible=past + ts, k_top=min(TOPK_MAX, (past + ts) // 4))
        att_o = _sample_attention(page_table, aq.astype(F32), sel[:, :past], sel[:, past:], ak, av,
                                  cache_k[l].reshape(n_pool, PAGE_SIZE, att_w),
                                  cache_v[l].reshape(n_pool, PAGE_SIZE, att_w), group=8, d_head=d_head)

        x1, h2, cw = outproj(hs, ret_o, att_o.astype(BF16), g1=g1, sc2=sc2, sh2=sh2, tm=db, rows_per_batch=1,
                             per_row=True)
        hs = moe(h2, cw, x1, g2, tm=db, rows_per_batch=1, per_row=True)
        outs[4].append(ak.reshape(db, ts, h_att, d_head))
        outs[5].append(av.reshape(db, ts, h_att, d_head))
        outs[6].append(iks.reshape(db, ts, d_idx))
        outs[7].append(ret_s)

    k_p, v_p, ik_p, r_p, k_s, v_s, ik_s, r_s = [jnp.stack(o) for o in outs]
    return (hp.reshape(b, t, d), hs.reshape(db, ts, d), k_p, v_p, ik_p, r_p, k_s, v_s, ik_s, r_s)
```

```python
import functools
import math

import numpy as np
import jax
import jax.numpy as jnp
from jax import lax
from jax.experimental import pallas as pl
from jax.experimental.pallas import tpu as pltpu

F32 = jnp.float32
BF16 = jnp.bfloat16
I32 = jnp.int32

PAGE_SIZE = 128
H_IDX = 8
TOPK_MAX = 256
TOP_K = 4
SWIGLU_LIMIT = 7.0
SWIGLU_ALPHA = 1.702
ROPE_THETA = 10000.0
EPS = 1e-6
INT_MIN = -(2 ** 31)
NEG_INF = float("-inf")

LANES = 128
VMEM_LIMIT = 56 * 1024 * 1024


def _cparams(sem):
    return pltpu.CompilerParams(dimension_semantics=sem, vmem_limit_bytes=VMEM_LIMIT)


def _dot(a, b):
    return jnp.dot(a, b, preferred_element_type=F32)


def _dot_nt(a, b):
    return lax.dot_general(a, b, (((1,), (1,)), ((), ())), preferred_element_type=F32)


def _sigmoid(x):
    return 1.0 / (1.0 + jnp.exp(-x))


def _dot_hp(a, b):
    return jnp.dot(a, b, preferred_element_type=F32, precision=lax.Precision.HIGHEST)


def _dot_nt_hp(a, b):
    return lax.dot_general(a, b, (((1,), (1,)), ((), ())), preferred_element_type=F32,
                           precision=lax.Precision.HIGHEST)


def _ordered_float(key):
    bits = jnp.where(key < 0, key ^ jnp.int32(0x7FFFFFFF), key)
    return pltpu.bitcast(bits, F32)


def _rope_halves(v, cos, sin_signed, first_half, width):
    swapped = jnp.where(first_half, pltpu.roll(v, width - 32, 1), pltpu.roll(v, 32, 1))
    return v * cos + swapped * sin_signed


def _inproj_body(x_ref, sc_ref, sh_ref, n1_ref, w_ref, wt_ref, cos_ref, sin_ref, qn_ref, kn_ref,
                 ikn_ref, gmat_ref,
                 rq_ref, rk_ref, rv_ref, rg_ref, aq_ref, ak_ref, akb_ref, av_ref, avb_ref, iq_ref,
                 tail_ref, *, seg_w, d_idx, iw_scale, k_scale, q_scale, precise):
    x = x_ref[...]
    tm = x.shape[0]
    ms = jnp.mean(x * x, axis=-1, keepdims=True)
    xn = x * lax.rsqrt(ms + EPS) * n1_ref[...]
    h = xn * (1.0 + sc_ref[...]) + sh_ref[...]
    if precise:
        mm = _dot_hp
    else:
        mm = _dot
        h = h.astype(BF16)

    cos1 = cos_ref[...]
    sin1 = sin_ref[...]
    reps = seg_w // LANES
    cos = jnp.concatenate([cos1] * reps, axis=1)
    sin = jnp.concatenate([sin1] * reps, axis=1)
    lane = lax.broadcasted_iota(I32, (tm, seg_w), 1)
    first = (lane & 32) == 0
    rope = functools.partial(_rope_halves, cos=cos, sin_signed=sin, first_half=first, width=seg_w)

    def seg(j):
        return mm(h, w_ref[:, j * seg_w:(j + 1) * seg_w])

    def head_norm(v, w):
        sq = v * v
        hi = sq.astype(BF16)
        lo = (sq - hi.astype(F32)).astype(BF16)
        msq = _dot(hi, gmat_ref[...]) + _dot(lo, gmat_ref[...])
        return v * lax.rsqrt(msq + EPS) * w

    rq_ref[...] = rope(seg(0)).astype(rq_ref.dtype)
    rk_ref[...] = (rope(seg(1)) * k_scale).astype(rk_ref.dtype)
    rv_ref[...] = seg(2).astype(rv_ref.dtype)
    rg_ref[...] = seg(3).astype(rg_ref.dtype)
    aq_ref[...] = (rope(head_norm(seg(4), qn_ref[...])) * q_scale).astype(aq_ref.dtype)
    ak = rope(head_norm(seg(5), kn_ref[...]))
    ak_ref[...] = ak
    akb_ref[...] = ak.astype(BF16)
    av = seg(6)
    av_ref[...] = av
    avb_ref[...] = av.astype(BF16)
    iq_ref[...] = rope(seg(7)).astype(iq_ref.dtype)

    zt = mm(h, wt_ref[...])
    lane_t = lax.broadcasted_iota(I32, (tm, LANES), 1)
    is_key = lane_t < d_idx
    msk = jnp.sum(jnp.where(is_key, zt * zt, 0.0), axis=-1, keepdims=True) * (1.0 / d_idx)
    kn = zt * lax.rsqrt(msk + EPS) * ikn_ref[...]
    kr = _rope_halves(kn, cos1, sin1, (lane_t & 32) == 0, LANES)
    tail_ref[...] = jnp.where(is_key, kr, zt * iw_scale)


def _inproj(x2d, sc, sh, n1, w_main, w_tail, cos, sin, qn, kn, ikn, gmat, *, tm, rows_per_batch,
            per_row, precise, d_ret, d_head, d_idx):
    act_dtype = F32 if precise else BF16
    n, d = x2d.shape
    seg_w = w_main.shape[1] // 8
    grid = (n // tm,)
    row = lambda i: (i, 0)
    const = lambda i: (0, 0)
    if per_row:
        mod_spec = pl.BlockSpec((tm, d), row)
        pos_spec = pl.BlockSpec((tm, LANES), row)
    else:
        tiles_per_batch = rows_per_batch // tm
        mod_spec = pl.BlockSpec((None, 1, d), lambda i: (i // tiles_per_batch, 0, 0))
        pos_spec = pl.BlockSpec((tm, LANES), lambda i: (i % tiles_per_batch, 0))
    in_specs = [
        pl.BlockSpec((tm, d), row), mod_spec, mod_spec, pl.BlockSpec((1, d), const),
        pl.BlockSpec(w_main.shape, const), pl.BlockSpec(w_tail.shape, const),
        pos_spec, pos_spec,
        pl.BlockSpec((1, seg_w), const), pl.BlockSpec((1, seg_w), const), pl.BlockSpec((1, LANES), const),
        pl.BlockSpec(gmat.shape, const),
    ]
    seg_spec = pl.BlockSpec((tm, seg_w), row)
    out_dtypes = [act_dtype, act_dtype, act_dtype, act_dtype, act_dtype, F32, BF16, F32, BF16, act_dtype]
    out_shape = [jax.ShapeDtypeStruct((n, seg_w), dt) for dt in out_dtypes]
    out_shape.append(jax.ShapeDtypeStruct((n, LANES), F32))
    out_specs = [seg_spec] * 10 + [pl.BlockSpec((tm, LANES), row)]
    body = functools.partial(_inproj_body, seg_w=seg_w, d_idx=d_idx, iw_scale=H_IDX ** -0.5,
                             k_scale=d_ret ** -0.5, q_scale=d_head ** -0.5, precise=precise)
    return pl.pallas_call(
        body, grid=grid, in_specs=in_specs, out_specs=out_specs, out_shape=out_shape,
        compiler_params=_cparams(("arbitrary",)), name="inproj",
    )(x2d, sc, sh, n1, w_main, w_tail, cos, sin, qn, kn, ikn, gmat)


def _group_norm_gate(o, g, gnw):
    mu = jnp.mean(o, axis=-1, keepdims=True)
    var = jnp.mean((o - mu) ** 2, axis=-1, keepdims=True)
    on = (o - mu) * lax.rsqrt(var + EPS) * gnw
    g = g.astype(F32)
    return g * _sigmoid(g) * on


def _ret_prompt_body(lg_ref, q_ref, kt_ref, v_ref, g_ref, gnw_ref, o_ref, s_ref, *, chunk):
    t, dk = q_ref.shape
    lg = lg_ref[pl.program_id(1)]
    ii = lax.broadcasted_iota(I32, (chunk, chunk), 0)
    jj = lax.broadcasted_iota(I32, (chunk, chunk), 1)
    rel = (ii - jj).astype(F32)
    decay = jnp.where(rel >= 0.0, jnp.exp(jnp.maximum(rel, 0.0) * lg), 0.0)
    q_dec = jnp.exp((lax.broadcasted_iota(I32, (chunk, 1), 0).astype(F32) + 1.0) * lg)
    k_dec = jnp.exp((chunk - 1.0 - lax.broadcasted_iota(I32, (1, chunk), 1).astype(F32)) * lg)
    g_chunk = jnp.exp(jnp.full((1, 1), chunk, F32) * lg)
    gnw = gnw_ref[...]
    state = jnp.zeros((dk, v_ref.shape[1]), F32)
    for c in range(t // chunk):
        rows = slice(c * chunk, (c + 1) * chunk)
        qc = q_ref[rows, :]
        ktc = kt_ref[:, rows]
        vc = v_ref[rows, :]
        inner = _dot(qc, ktc) * decay
        o = _dot(inner.astype(BF16), vc) + _dot(qc, state.astype(BF16)) * q_dec
        state = state * g_chunk + _dot((ktc.astype(F32) * k_dec).astype(BF16), vc)
        o_ref[rows, :] = _group_norm_gate(o, g_ref[rows, :], gnw).astype(o_ref.dtype)
    s_ref[...] = state


def _ret_prompt(log_g, q, kt, v, g, gnw, *, chunk):
    b, h, t, dk = q.shape
    dv = v.shape[-1]
    blk = lambda r, c: pl.BlockSpec((None, None, r, c), lambda i, j, lg: (i, j, 0, 0))
    grid_spec = pltpu.PrefetchScalarGridSpec(
        num_scalar_prefetch=1, grid=(b, h),
        in_specs=[blk(t, dk), blk(dk, t), blk(t, dv), blk(t, dv),
                  pl.BlockSpec((None, 1, dv), lambda i, j, lg: (j, 0, 0))],
        out_specs=[blk(t, dv), blk(dk, dv)],
    )
    return pl.pallas_call(
        functools.partial(_ret_prompt_body, chunk=chunk), grid_spec=grid_spec,
        out_shape=[jax.ShapeDtypeStruct((b, h, t, dv), BF16), jax.ShapeDtypeStruct((b, h, dk, dv), F32)],
        compiler_params=_cparams(("arbitrary", "arbitrary")), name="ret_prompt",
    )(log_g, q, kt, v, g, gnw)


def _ret_sample_body(lg_ref, qc_ref, kc_ref, v_ref, g_ref, s0_ref, gnw_ref, o_ref, s_ref):
    for h in range(qc_ref.shape[0]):
        gamma = jnp.exp(jnp.full((1, 1), 1.0, F32) * lg_ref[h])
        qc = qc_ref[h]
        kc = kc_ref[h]
        v = v_ref[h]
        s0 = s0_ref[h]
        qk = jnp.sum(qc * kc, axis=0, keepdims=True)
        o = qk * v + jnp.sum(qc * s0, axis=0, keepdims=True) * gamma
        s_ref[h] = s0 * gamma + kc * v
        o_ref[h] = _group_norm_gate(o, g_ref[h], gnw_ref[h]).astype(o_ref.dtype)


def _ret_sample(log_g, q, k, v, g, s0, gnw):
    db, h, dk = q.shape
    dv = v.shape[-1]
    col = pl.BlockSpec((None, h, dk, 1), lambda i, lg: (i, 0, 0, 0))
    rowv = pl.BlockSpec((None, h, 1, dv), lambda i, lg: (i, 0, 0, 0))
    mat = pl.BlockSpec((None, h, dk, dv), lambda i, lg: (i, 0, 0, 0))
    grid_spec = pltpu.PrefetchScalarGridSpec(
        num_scalar_prefetch=1, grid=(db,),
        in_specs=[col, col, rowv, rowv, mat, pl.BlockSpec((h, 1, dv), lambda i, lg: (0, 0, 0))],
        out_specs=[rowv, mat],
    )
    o, s = pl.pallas_call(
        _ret_sample_body, grid_spec=grid_spec,
        out_shape=[jax.ShapeDtypeStruct((db, h, 1, dv), F32), jax.ShapeDtypeStruct((db, h, dk, dv), F32)],
        compiler_params=_cparams(("arbitrary",)), name="ret_sample",
    )(log_g, q.reshape(db, h, dk, 1), k.reshape(db, h, dk, 1), v.reshape(db, h, 1, dv),
      g.reshape(db, h, 1, dv), s0, gnw)
    return o.reshape(db, h * dv), s


def _select_topk(sc_ref, pos, n_visible, k_top, n_pos_bits, axis):
    def count(mask):
        return jnp.sum(jnp.where(mask, 1.0, 0.0), axis=axis, keepdims=True)

    kf = float(k_top)
    c0 = count(sc_ref[...] >= 0.0)
    cur = jnp.where(c0 >= kf, jnp.int32(0), jnp.int32(INT_MIN))

    def value_step(i, cur):
        cand = cur + (jnp.int32(1) << (30 - i))
        return jnp.where(count(sc_ref[...] >= _ordered_float(cand)) >= kf, cand, cur)

    thr = _ordered_float(lax.fori_loop(0, 31, value_step, cur))
    need = kf - count(sc_ref[...] > thr)

    def tie_step(i, m):
        cand = m + (jnp.int32(1) << (n_pos_bits - 1 - i))
        below = count((sc_ref[...] == thr) & (pos < cand))
        return jnp.where(below < need, cand, m)

    cut = lax.fori_loop(0, n_pos_bits, tie_step, jnp.zeros(thr.shape, I32))
    sc = sc_ref[...]
    picked = (sc > thr) | ((sc == thr) & (pos <= cut))
    take_all = jnp.broadcast_to(n_visible, sc.shape) <= k_top
    return (take_all & (sc > NEG_INF)) | (~take_all & picked)


def _dsa_prompt_body(ik_ref, iqt_ref, iwt_ref, aqt_ref, k_ref, vt_ref, ot_ref, sc_ref, bias_ref,
                     *, k_top, d_idx, d_head):
    t, qb = sc_ref.shape
    j = pl.program_id(1)
    krow = lax.broadcasted_iota(I32, (t, qb), 0)
    qpos = j * qb + lax.broadcasted_iota(I32, (1, qb), 1)

    ik = ik_ref[...]
    sc = jnp.zeros((t, qb), F32)
    for h in range(iwt_ref.shape[0]):
        s = _dot(ik, iqt_ref[h * d_idx:(h + 1) * d_idx, :])
        sc = sc + jnp.maximum(s, 0.0) * iwt_ref[h:h + 1, :]
    sc_ref[...] = jnp.where(krow <= qpos, sc, NEG_INF)
    sel = _select_topk(sc_ref, krow, qpos + 1, k_top, int(math.log2(t)), 0)
    bias_ref[...] = jnp.where(sel, 0.0, NEG_INF)

    for h in range(k_ref.shape[0]):
        rows = slice(h * d_head, (h + 1) * d_head)
        s = _dot(k_ref[h], aqt_ref[rows, :]) + bias_ref[...]
        m = jnp.max(s, axis=0, keepdims=True)
        p = jnp.exp(s - m)
        l = jnp.sum(p, axis=0, keepdims=True)
        o = _dot(vt_ref[rows, :], p.astype(BF16))
        ot_ref[rows, :] = (o / l).astype(ot_ref.dtype)


def _dsa_prompt(ik, iqt, iwt, aqt, kh, vt, *, qb, k_top):
    b, t, d_idx = ik.shape
    n_heads, d_head = kh.shape[1], kh.shape[3]
    w = vt.shape[1]
    per_b = lambda r, c: pl.BlockSpec((None, r, c), lambda i, j: (i, 0, 0))
    per_q = lambda r: pl.BlockSpec((None, r, qb), lambda i, j: (i, 0, j))
    return pl.pallas_call(
        functools.partial(_dsa_prompt_body, k_top=k_top, d_idx=d_idx, d_head=d_head),
        grid=(b, t // qb),
        in_specs=[per_b(t, d_idx), per_q(iqt.shape[1]), per_q(iwt.shape[1]), per_q(w),
                  pl.BlockSpec((None, n_heads, t, d_head), lambda i, j: (i, 0, 0, 0)), per_b(w, t)],
        out_specs=per_q(w),
        out_shape=jax.ShapeDtypeStruct((b, w, t), BF16),
        scratch_shapes=[pltpu.VMEM((t, qb), F32), pltpu.VMEM((t, qb), F32)],
        compiler_params=_cparams(("arbitrary", "arbitrary")), name="dsa_prompt",
    )(ik, iqt, iwt, aqt, kh, vt)


def _page_copies(cache_ref, buf_ref, sem_ref, pt_ref, seq, first_page, n_pages, slot):
    return [pltpu.make_async_copy(cache_ref.at[pt_ref[seq, first_page + p]],
                                  buf_ref.at[slot, pl.ds(p * PAGE_SIZE, PAGE_SIZE)],
                                  sem_ref.at[slot])
            for p in range(n_pages)]


def _sidx_body(pt_ref, qi_ref, iw_ref, iknew_ref, cache_ref, out_ref, buf_ref, sem_ref, *, n_pages):
    b = pl.program_id(0)
    nb = pl.num_programs(0)
    slot = b % 2
    past = n_pages * PAGE_SIZE

    def start(seq, s):
        for c in _page_copies(cache_ref, buf_ref, sem_ref, pt_ref, seq, 0, n_pages, s):
            c.start()

    @pl.when(b == 0)
    def _():
        buf_ref[:, past:, :] = jnp.zeros((2, LANES, buf_ref.shape[2]), F32)
        start(0, 0)

    @pl.when(b + 1 < nb)
    def _():
        start(b + 1, 1 - slot)

    for c in _page_copies(cache_ref, buf_ref, sem_ref, pt_ref, b, 0, n_pages, slot):
        c.wait()
    buf_ref[slot, past:past + 1, :] = iknew_ref[...]

    s = _dot_nt_hp(qi_ref[...], buf_ref[slot])
    sc = jnp.sum(jnp.maximum(s, 0.0) * iw_ref[...], axis=0, keepdims=True)
    out_ref[...] = sc


def _sample_index_scores(page_table, qi, iw, ik_new, cache_idx_k):
    db, n_pages = page_table.shape
    d_idx = cache_idx_k.shape[-1]
    n_keys = n_pages * PAGE_SIZE + LANES
    grid_spec = pltpu.PrefetchScalarGridSpec(
        num_scalar_prefetch=1, grid=(db,),
        in_specs=[pl.BlockSpec((None, H_IDX, d_idx), lambda i, pt: (i, 0, 0)),
                  pl.BlockSpec((None, H_IDX, 1), lambda i, pt: (i, 0, 0)),
                  pl.BlockSpec((None, 1, d_idx), lambda i, pt: (i, 0, 0)),
                  pl.BlockSpec(memory_space=pl.ANY)],
        out_specs=pl.BlockSpec((None, 1, n_keys), lambda i, pt: (i, 0, 0)),
        scratch_shapes=[pltpu.VMEM((2, n_keys, d_idx), F32), pltpu.SemaphoreType.DMA((2,))],
    )
    out = pl.pallas_call(
        functools.partial(_sidx_body, n_pages=n_pages), grid_spec=grid_spec,
        out_shape=jax.ShapeDtypeStruct((db, 1, n_keys), F32),
        compiler_params=_cparams(("arbitrary",)), name="sample_index_scores",
    )(page_table, qi, iw, ik_new, cache_idx_k)
    return out.reshape(db, n_keys)


def _ssel_body(sc_ref, idx_ref, work_ref, *, n_visible, k_top):
    n, w = sc_ref.shape
    pos = lax.broadcasted_iota(I32, (n, w), 1)
    work_ref[...] = jnp.where(pos < n_visible, sc_ref[...], NEG_INF)
    sel = _select_topk(work_ref, pos, jnp.int32(n_visible), k_top, int(math.ceil(math.log2(w))), 1)

    run = jnp.where(sel, 1.0, 0.0)
    shift = 1
    while shift < w:
        run = run + jnp.where(pos >= shift, pltpu.roll(run, shift, 1), 0.0)
        shift *= 2
    work_ref[...] = jnp.where(sel, run - 1.0, -1.0)

    slot = lax.broadcasted_iota(I32, (k_top, w), 0).astype(F32)
    posf = lax.broadcasted_iota(I32, (k_top, w), 1).astype(F32)

    def compact(s, carry):
        hit = work_ref[pl.ds(s, 1), :] == slot
        idx_ref[s] = jnp.sum(jnp.where(hit, posf, 0.0), axis=-1, keepdims=True).astype(I32)
        return carry

    lax.fori_loop(0, n, compact, 0)


def _sample_select(scores, *, n_visible, k_top):
    n = scores.shape[0]
    idx = pl.pallas_call(
        functools.partial(_ssel_body, n_visible=n_visible, k_top=k_top),
        out_shape=jax.ShapeDtypeStruct((n, k_top, 1), I32),
        scratch_shapes=[pltpu.VMEM(scores.shape, F32)],
        compiler_params=pltpu.CompilerParams(vmem_limit_bytes=VMEM_LIMIT), name="sample_select",
    )(scores)
    return idx.reshape(n, k_top)


def _sattn_body(pt_ref, idx_ref, q_ref, knew_ref, vnew_ref, ck_ref, cv_ref, o_ref,
                kbuf_ref, vbuf_ref, ksem_ref, vsem_ref, *, past):
    b = pl.program_id(0)
    nb = pl.num_programs(0)
    slot = b % 2
    k_top = kbuf_ref.shape[1]

    def row_copies(seq, s, r, src_k, src_v):
        return (pltpu.make_async_copy(src_k, kbuf_ref.at[s, r], ksem_ref.at[s]),
                pltpu.make_async_copy(src_v, vbuf_ref.at[s, r], vsem_ref.at[s]))

    def start(seq, s):
        def issue(r, carry):
            i = idx_ref[seq, r]

            @pl.when(i < past)
            def _():
                page = pt_ref[seq, lax.shift_right_logical(i, 7)]
                off = i & (PAGE_SIZE - 1)
                for c in row_copies(seq, s, r, ck_ref.at[page, off], cv_ref.at[page, off]):
                    c.start()

            @pl.when(i >= past)
            def _():
                for c in row_copies(seq, s, r, knew_ref.at[seq], vnew_ref.at[seq]):
                    c.start()

            return carry

        lax.fori_loop(0, k_top, issue, 0)

    @pl.when(b == 0)
    def _():
        start(0, 0)

    @pl.when(b + 1 < nb)
    def _():
        start(b + 1, 1 - slot)

    def wait(r, carry):
        for c in row_copies(b, slot, r, ck_ref.at[0, 0], cv_ref.at[0, 0]):
            c.wait()
        return carry

    lax.fori_loop(0, k_top, wait, 0)

    k = kbuf_ref[slot]
    v = vbuf_ref[slot]
    s = jnp.sum(k * q_ref[...][None], axis=-1, keepdims=True)
    m = jnp.max(s, axis=0, keepdims=True)
    p = jnp.exp(s - m)
    l = jnp.sum(p, axis=0)
    o_ref[...] = jnp.sum(p * v, axis=0) / l


def _sample_attention(page_table, idx, q, k_new, v_new, cache_k, cache_v):
    db, n_heads, d_head = q.shape
    k_top = idx.shape[1]
    assert PAGE_SIZE == 1 << 7
    per_seq = pl.BlockSpec((None, n_heads, d_head), lambda i, pt, ix: (i, 0, 0))
    any_spec = pl.BlockSpec(memory_space=pl.ANY)
    grid_spec = pltpu.PrefetchScalarGridSpec(
        num_scalar_prefetch=2, grid=(db,),
        in_specs=[per_seq, any_spec, any_spec, any_spec, any_spec],
        out_specs=per_seq,
        scratch_shapes=[pltpu.VMEM((2, k_top, n_heads, d_head), F32), pltpu.VMEM((2, k_top, n_heads, d_head), F32),
                        pltpu.SemaphoreType.DMA((2,)), pltpu.SemaphoreType.DMA((2,))],
    )
    return pl.pallas_call(
        functools.partial(_sattn_body, past=page_table.shape[1] * PAGE_SIZE), grid_spec=grid_spec,
        out_shape=jax.ShapeDtypeStruct((db, n_heads, d_head), F32),
        compiler_params=_cparams(("arbitrary",)), name="sample_attention",
    )(page_table, idx, q, k_new, v_new, cache_k, cache_v)


def _outproj_body(x_ref, ret_ref, att_ref, wo_ref, g1_ref, sc2_ref, sh2_ref, n2_ref, rwh_ref, rwl_ref,
                  rb_ref, x1_ref, h2_ref, cw_ref, *, n_experts, precise):
    half = ret_ref.shape[1]
    mm = _dot_hp if precise else _dot
    y = mm(ret_ref[...], wo_ref[:half, :]) + mm(att_ref[...], wo_ref[half:, :])
    x1 = x_ref[...] + g1_ref[...] * y
    x1_ref[...] = x1
    ms = jnp.mean(x1 * x1, axis=-1, keepdims=True)
    h2 = x1 * lax.rsqrt(ms + EPS) * n2_ref[...] * (1.0 + sc2_ref[...]) + sh2_ref[...]
    hi = h2.astype(BF16)
    h2_ref[...] = hi
    lo = (h2 - hi.astype(F32)).astype(BF16)
    logits = _dot(hi, rwh_ref[...]) + _dot(lo, rwh_ref[...]) + _dot(hi, rwl_ref[...]) + rb_ref[...]

    tm = logits.shape[0]
    lane = lax.broadcasted_iota(I32, (tm, LANES), 1).astype(F32)
    work = jnp.where(lane < n_experts, logits, NEG_INF)
    top_vals, top_hot = [], []
    for _ in range(TOP_K):
        m = jnp.max(work, axis=-1, keepdims=True)
        first = jnp.min(jnp.where(work == m, lane, float(LANES)), axis=-1, keepdims=True)
        hot = lane == first
        top_vals.append(m)
        top_hot.append(hot)
        work = jnp.where(hot, NEG_INF, work)
    exps = [jnp.exp(v - top_vals[0]) for v in top_vals]
    den = exps[0]
    for e in exps[1:]:
        den = den + e
    cw = jnp.zeros((tm, LANES), F32)
    for e, hot in zip(exps, top_hot):
        cw = cw + jnp.where(hot, e / den, 0.0)
    cw_ref[...] = cw


def _outproj(x2d, ret, att, wo, g1, sc2, sh2, n2, rwh, rwl, rb, *, tm, rows_per_batch, per_row, n_experts,
             precise):
    n, d = x2d.shape
    half = ret.shape[1]
    row = lambda i: (i, 0)
    const = lambda i: (0, 0)
    if per_row:
        mod_spec = pl.BlockSpec((tm, d), row)
    else:
        tiles_per_batch = rows_per_batch // tm
        mod_spec = pl.BlockSpec((None, 1, d), lambda i: (i // tiles_per_batch, 0, 0))
    return pl.pallas_call(
        functools.partial(_outproj_body, n_experts=n_experts, precise=precise), grid=(n // tm,),
        in_specs=[pl.BlockSpec((tm, d), row), pl.BlockSpec((tm, half), row), pl.BlockSpec((tm, half), row),
                  pl.BlockSpec(wo.shape, const), mod_spec, mod_spec, mod_spec, pl.BlockSpec((1, d), const),
                  pl.BlockSpec(rwh.shape, const), pl.BlockSpec(rwl.shape, const), pl.BlockSpec((1, LANES), const)],
        out_specs=[pl.BlockSpec((tm, d), row), pl.BlockSpec((tm, d), row), pl.BlockSpec((tm, LANES), row)],
        out_shape=[jax.ShapeDtypeStruct((n, d), F32), jax.ShapeDtypeStruct((n, d), BF16),
                   jax.ShapeDtypeStruct((n, LANES), F32)],
        compiler_params=_cparams(("arbitrary",)), name="outproj",
    )(x2d, ret, att, wo, g1, sc2, sh2, n2, rwh, rwl, rb)


def _moe_body(h_ref, cw_ref, x1_ref, g2_ref, wgu_ref, bgu_ref, wd_ref, bd_ref, out_ref, acc_ref, *, f_chunk):
    e = pl.program_id(1)
    tm = h_ref.shape[0]
    d_ff = wd_ref.shape[0]

    @pl.when(e == 0)
    def _():
        acc_ref[...] = jnp.zeros(acc_ref.shape, F32)

    lane = lax.broadcasted_iota(I32, (tm, LANES), 1)
    c = jnp.sum(jnp.where(lane == e, cw_ref[...], 0.0), axis=-1, keepdims=True)
    h = h_ref[...]
    y = jnp.zeros(acc_ref.shape, F32)
    for j in range(d_ff // f_chunk):
        cols = slice(j * f_chunk, (j + 1) * f_chunk)
        ucols = slice(d_ff + j * f_chunk, d_ff + (j + 1) * f_chunk)
        hg = _dot(h, wgu_ref[:, cols]) + bgu_ref[:, cols]
        hu = _dot(h, wgu_ref[:, ucols]) + bgu_ref[:, ucols]
        gate = jnp.minimum(hg, SWIGLU_LIMIT)
        up = jnp.clip(hu, -SWIGLU_LIMIT, SWIGLU_LIMIT)
        act = (up + 1.0) * gate * _sigmoid(SWIGLU_ALPHA * gate)
        y = y + _dot(act.astype(BF16), wd_ref[cols, :])
    acc_ref[...] += c * (y + bd_ref[...])

    @pl.when(e == pl.num_programs(1) - 1)
    def _():
        out_ref[...] = x1_ref[...] + g2_ref[...] * acc_ref[...]


def _moe(h2, cw, x1, g2, wgu, bgu, wd, bd, *, tm, rows_per_batch, per_row, f_chunk):
    n, d = h2.shape
    n_exp, _, two_f = wgu.shape
    d_ff = two_f // 2
    row = lambda i, e: (i, 0)
    if per_row:
        mod_spec = pl.BlockSpec((tm, d), row)
    else:
        tiles_per_batch = rows_per_batch // tm
        mod_spec = pl.BlockSpec((None, 1, d), lambda i, e: (i // tiles_per_batch, 0, 0))
    return pl.pallas_call(
        functools.partial(_moe_body, f_chunk=f_chunk), grid=(n // tm, n_exp),
        in_specs=[pl.BlockSpec((tm, d), row), pl.BlockSpec((tm, LANES), row), pl.BlockSpec((tm, d), row), mod_spec,
                  pl.BlockSpec((None, d, two_f), lambda i, e: (e, 0, 0)),
                  pl.BlockSpec((None, 1, two_f), lambda i, e: (e, 0, 0)),
                  pl.BlockSpec((None, d_ff, d), lambda i, e: (e, 0, 0)),
                  pl.BlockSpec((None, 1, d), lambda i, e: (e, 0, 0))],
        out_specs=pl.BlockSpec((tm, d), row),
        out_shape=jax.ShapeDtypeStruct((n, d), F32),
        scratch_shapes=[pltpu.VMEM((tm, d), F32)],
        compiler_params=_cparams(("arbitrary", "arbitrary")), name="moe",
    )(h2, cw, x1, g2, wgu, bgu, wd, bd)


def _rope_tables(pos, d):
    inv = jnp.power(ROPE_THETA, -jnp.arange(0, d, 2, dtype=F32) / d)
    ang = pos.astype(F32)[:, None] * inv[None, :]
    cos = jnp.cos(jnp.concatenate([ang, ang], axis=-1))
    sin = jnp.sin(ang)
    sin = jnp.concatenate([-sin, sin], axis=-1)
    reps = LANES // d
    return jnp.tile(cos, (1, reps)), jnp.tile(sin, (1, reps))


def _split_bf16(x):
    hi = x.astype(BF16)
    return hi, (x - hi.astype(F32)).astype(BF16)


def _adaln_body(c_ref, w_ref, b_ref, o_ref):
    c = c_ref[...]
    a_hi, a_lo = _split_bf16(c * _sigmoid(c))
    w_hi, w_lo = _split_bf16(w_ref[...])
    o_ref[...] = _dot(a_hi, w_hi) + _dot(a_lo, w_hi) + _dot(a_hi, w_lo) + b_ref[...]


def _adaln(c, ada_w, ada_b):
    n, d = c.shape
    n_out = ada_w.shape[1]
    tn = d
    out = pl.pallas_call(
        _adaln_body, grid=(n_out // tn,),
        in_specs=[pl.BlockSpec((n, d), lambda j: (0, 0)), pl.BlockSpec((d, tn), lambda j: (0, j)),
                  pl.BlockSpec((1, tn), lambda j: (0, j))],
        out_specs=pl.BlockSpec((n, tn), lambda j: (0, j)),
        out_shape=jax.ShapeDtypeStruct((n, n_out), F32),
        compiler_params=_cparams(("arbitrary",)), name="adaln",
    )(c, ada_w, ada_b[None])
    return jnp.split(out, 6, axis=-1)


def kernel(x_prompt, x_sample, cache_k, cache_v, cache_idx_k, state_ret, page_table, c_prompt, c_sample,
           ada_w, ada_b, norm1_w, w_in, q_norm_w, k_norm_w, idx_k_norm_w, ret_gn_w, w_out, norm2_w,
           router_w, router_b, w_gate_up, b_gate_up, w_down, b_down):
    depth = w_in.shape[0]
    b, t, d = x_prompt.shape
    db, ts, _ = x_sample.shape
    assert ts == 1, "the sample group decodes one token per sequence"
    _, _, h_ret, d_ret, _ = state_ret.shape
    _, n_pool, page, h_att, d_head = cache_k.shape
    assert page == PAGE_SIZE
    d_idx = cache_idx_k.shape[-1]
    ret_w, att_w = h_ret * d_ret, h_att * d_head
    assert ret_w == att_w == H_IDX * d_idx and d_ret == d_head == d_idx == 64
    n_experts = router_w.shape[-1]
    n_pages = page_table.shape[1]
    past = n_pages * PAGE_SIZE
    n_main = 8 * ret_w

    cos_p, sin_p = _rope_tables(jnp.arange(t, dtype=jnp.int32), d_ret)
    cos_s, sin_s = _rope_tables(jnp.full((db,), past, jnp.int32), d_ret)
    log_g = jnp.log1p(-jnp.power(2.0, -5.0 - jnp.arange(h_ret, dtype=F32)))
    gmat = jnp.asarray(np.kron(np.eye(ret_w // d_head), np.full((d_head, d_head), 1.0 / d_head)), BF16)

    hp, hs = x_prompt.reshape(b * t, d), x_sample.reshape(db, d)
    outs = [[] for _ in range(8)]
    for l in range(depth):
        w_main = w_in[l, :, :n_main]
        w_tail = jnp.pad(w_in[l, :, n_main:], ((0, 0), (0, LANES - (w_in.shape[2] - n_main))))
        qn = jnp.tile(q_norm_w[l], h_att)[None]
        kn = jnp.tile(k_norm_w[l], h_att)[None]
        ikn = jnp.pad(idx_k_norm_w[l], (0, LANES - d_idx))[None]
        gnw = ret_gn_w[l].reshape(h_ret, 1, d_ret)
        wo = w_out[l]
        rw = jnp.pad(router_w[l], ((0, 0), (0, LANES - n_experts)))
        rwh = rw.astype(BF16)
        rwl = (rw - rwh.astype(F32)).astype(BF16)
        rb = jnp.pad(router_b[l], (0, LANES - n_experts))[None]
        n1, n2 = norm1_w[l][None], norm2_w[l][None]
        wgu, wd = w_gate_up[l].astype(BF16), w_down[l].astype(BF16)
        bgu, bd = b_gate_up[l][:, None, :], b_down[l][:, None, :]
        inproj = functools.partial(_inproj, n1=n1, qn=qn, kn=kn, ikn=ikn, gmat=gmat,
                                   d_ret=d_ret, d_head=d_head, d_idx=d_idx)
        outproj = functools.partial(_outproj, n2=n2, rwh=rwh, rwl=rwl, rb=rb, n_experts=n_experts)
        moe = functools.partial(_moe, wgu=wgu, bgu=bgu, wd=wd, bd=bd, f_chunk=512)

        mods = _adaln(jnp.concatenate([c_prompt, c_sample], axis=0), ada_w[l], ada_b[l])

        sh1, sc1, g1, sh2, sc2, g2 = [a[:b, None, :] for a in mods]
        rq, rk, rv, rg, aq, ak, akb, av, avb, iq, tail = inproj(
            hp, sc1, sh1, w_main=w_main.astype(BF16), w_tail=w_tail.astype(BF16), cos=cos_p, sin=sin_p,
            tm=256, rows_per_batch=t, per_row=False, precise=False)

        heads = lambda a, nh: a.reshape(b, t, nh, -1).transpose(0, 2, 1, 3)
        ret_o, ret_s = _ret_prompt(log_g, heads(rq, h_ret), heads(rk, h_ret).swapaxes(2, 3), heads(rv, h_ret),
                                   heads(rg, h_ret), gnw, chunk=256)
        ret_o = ret_o.transpose(0, 2, 1, 3).reshape(b * t, ret_w)

        tail3 = tail.reshape(b, t, LANES)
        ikp = tail3[:, :, :d_idx]
        seq_t = lambda a: a.reshape(b, t, -1).swapaxes(1, 2)
        iwt = tail3[:, :, d_idx:d_idx + H_IDX].swapaxes(1, 2) * (d_idx ** -0.5)
        att_t = _dsa_prompt(ikp.astype(BF16), seq_t(iq), iwt, seq_t(aq), heads(akb, h_att), seq_t(avb),
                            qb=128, k_top=min(TOPK_MAX, t // 4))
        att_o = att_t.swapaxes(1, 2).reshape(b * t, att_w)

        x1, h2, cw = outproj(hp, ret_o, att_o, wo=wo.astype(BF16), g1=g1, sc2=sc2, sh2=sh2, tm=512,
                             rows_per_batch=t, per_row=False, precise=False)
        hp = moe(h2, cw, x1, g2, tm=min(1024, t), rows_per_batch=t, per_row=False)
        outs[0].append(ak.reshape(b, t, h_att, d_head))
        outs[1].append(av.reshape(b, t, h_att, d_head))
        outs[2].append(ikp)
        outs[3].append(ret_s)

        sh1, sc1, g1, sh2, sc2, g2 = [a[b:] for a in mods]
        rq, rk, rv, rg, aq, ak, akb, av, avb, iq, tail = inproj(
            hs, sc1, sh1, w_main=w_main, w_tail=w_tail, cos=cos_s, sin=sin_s, tm=db, rows_per_batch=1,
            per_row=True, precise=True)
        hd = lambda a: a.reshape(db, h_ret, d_ret)
        ret_o, ret_s = _ret_sample(log_g, hd(rq), hd(rk), hd(rv), hd(rg), state_ret[l], gnw)

        iks = tail[:, :d_idx]
        iw = tail[:, d_idx:d_idx + H_IDX] * (d_idx ** -0.5)
        scores = _sample_index_scores(page_table, iq.reshape(db, H_IDX, d_idx), iw[:, :, None], iks[:, None, :],
                                      cache_idx_k[l])
        idx = _sample_select(scores, n_visible=past + ts, k_top=min(TOPK_MAX, (past + ts) // 4))
        hda = lambda a: a.reshape(db, h_att, d_head)
        att_o = _sample_attention(page_table, idx, hda(aq), hda(ak), hda(av), cache_k[l], cache_v[l])

        x1, h2, cw = outproj(hs, ret_o, att_o.reshape(db, att_w), wo=wo, g1=g1, sc2=sc2, sh2=sh2, tm=db,
                             rows_per_batch=1, per_row=True, precise=True)
        hs = moe(h2, cw, x1, g2, tm=db, rows_per_batch=1, per_row=True)
        outs[4].append(ak.reshape(db, ts, h_att, d_head))
        outs[5].append(av.reshape(db, ts, h_att, d_head))
        outs[6].append(iks.reshape(db, ts, d_idx))
        outs[7].append(ret_s)

    k_p, v_p, ik_p, r_p, k_s, v_s, ik_s, r_s = [jnp.stack(o) for o in outs]
    return (hp.reshape(b, t, d), hs.reshape(db, ts, d), k_p, v_p, ik_p, r_p, k_s, v_s, ik_s, r_s)
```

```python
import functools
import math

import numpy as np
import jax
import jax.numpy as jnp
from jax import lax
from jax.experimental import pallas as pl
from jax.experimental.pallas import tpu as pltpu

F32 = jnp.float32
BF16 = jnp.bfloat16
I32 = jnp.int32

PAGE_SIZE = 128
H_IDX = 8
TOPK_MAX = 256
TOP_K = 4
SWIGLU_LIMIT = 7.0
SWIGLU_ALPHA = 1.702
ROPE_THETA = 10000.0
EPS = 1e-6
INT_MIN = -(2 ** 31)
NEG_INF = float("-inf")

LANES = 128
VMEM_LIMIT = 56 * 1024 * 1024


def _cparams(sem):
    return pltpu.CompilerParams(dimension_semantics=sem, vmem_limit_bytes=VMEM_LIMIT)


def _dot(a, b):
    return jnp.dot(a, b, preferred_element_type=F32)


def _dot_nt(a, b):
    return lax.dot_general(a, b, (((1,), (1,)), ((), ())), preferred_element_type=F32)


def _sigmoid(x):
    return 1.0 / (1.0 + jnp.exp(-x))


def _dot_hp(a, b):
    return jnp.dot(a, b, preferred_element_type=F32, precision=lax.Precision.HIGHEST)


def _dot_nt_hp(a, b):
    return lax.dot_general(a, b, (((1,), (1,)), ((), ())), preferred_element_type=F32,
                           precision=lax.Precision.HIGHEST)


def _ordered_float(key):
    bits = jnp.where(key < 0, key ^ jnp.int32(0x7FFFFFFF), key)
    return pltpu.bitcast(bits, F32)


def _rope_halves(v, cos, sin_signed, first_half, width):
    swapped = jnp.where(first_half, pltpu.roll(v, width - 32, 1), pltpu.roll(v, 32, 1))
    return v * cos + swapped * sin_signed


def _inproj_body(x_ref, sc_ref, sh_ref, n1_ref, w_ref, wt_ref, cos_ref, sin_ref, qn_ref, kn_ref,
                 ikn_ref, gmat_ref,
                 rq_ref, rk_ref, rv_ref, rg_ref, aq_ref, ak_ref, akb_ref, av_ref, avb_ref, iq_ref,
                 tail_ref, *, seg_w, d_idx, iw_scale, k_scale, q_scale, precise):
    x = x_ref[...]
    tm = x.shape[0]
    ms = jnp.mean(x * x, axis=-1, keepdims=True)
    xn = x * lax.rsqrt(ms + EPS) * n1_ref[...]
    h = xn * (1.0 + sc_ref[...]) + sh_ref[...]
    if precise:
        mm = _dot_hp
    else:
        mm = _dot
        h = h.astype(BF16)

    cos1 = cos_ref[...]
    sin1 = sin_ref[...]
    reps = seg_w // LANES
    cos = jnp.concatenate([cos1] * reps, axis=1)
    sin = jnp.concatenate([sin1] * reps, axis=1)
    lane = lax.broadcasted_iota(I32, (tm, seg_w), 1)
    first = (lane & 32) == 0
    rope = functools.partial(_rope_halves, cos=cos, sin_signed=sin, first_half=first, width=seg_w)

    def seg(j):
        return mm(h, w_ref[:, j * seg_w:(j + 1) * seg_w])

    def head_norm(v, w):
        sq = v * v
        hi = sq.astype(BF16)
        lo = (sq - hi.astype(F32)).astype(BF16)
        msq = _dot(hi, gmat_ref[...]) + _dot(lo, gmat_ref[...])
        return v * lax.rsqrt(msq + EPS) * w

    rq_ref[...] = rope(seg(0)).astype(rq_ref.dtype)
    rk_ref[...] = (rope(seg(1)) * k_scale).astype(rk_ref.dtype)
    rv_ref[...] = seg(2).astype(rv_ref.dtype)
    rg_ref[...] = seg(3).astype(rg_ref.dtype)
    aq_ref[...] = (rope(head_norm(seg(4), qn_ref[...])) * q_scale).astype(aq_ref.dtype)
    ak = rope(head_norm(seg(5), kn_ref[...]))
    ak_ref[...] = ak
    akb_ref[...] = ak.astype(BF16)
    av = seg(6)
    av_ref[...] = av
    avb_ref[...] = av.astype(BF16)
    iq_ref[...] = rope(seg(7)).astype(iq_ref.dtype)

    zt = mm(h, wt_ref[...])
    lane_t = lax.broadcasted_iota(I32, (tm, LANES), 1)
    is_key = lane_t < d_idx
    msk = jnp.sum(jnp.where(is_key, zt * zt, 0.0), axis=-1, keepdims=True) * (1.0 / d_idx)
    kn = zt * lax.rsqrt(msk + EPS) * ikn_ref[...]
    kr = _rope_halves(kn, cos1, sin1, (lane_t & 32) == 0, LANES)
    tail_ref[...] = jnp.where(is_key, kr, zt * iw_scale)


def _inproj(x2d, sc, sh, n1, w_main, w_tail, cos, sin, qn, kn, ikn, gmat, *, tm, rows_per_batch,
            per_row, precise, d_ret, d_head, d_idx):
    act_dtype = F32 if precise else BF16
    n, d = x2d.shape
    seg_w = w_main.shape[1] // 8
    grid = (n // tm,)
    row = lambda i: (i, 0)
    const = lambda i: (0, 0)
    if per_row:
        mod_spec = pl.BlockSpec((tm, d), row)
        pos_spec = pl.BlockSpec((tm, LANES), row)
    else:
        tiles_per_batch = rows_per_batch // tm
        mod_spec = pl.BlockSpec((None, 1, d), lambda i: (i // tiles_per_batch, 0, 0))
        pos_spec = pl.BlockSpec((tm, LANES), lambda i: (i % tiles_per_batch, 0))
    in_specs = [
        pl.BlockSpec((tm, d), row), mod_spec, mod_spec, pl.BlockSpec((1, d), const),
        pl.BlockSpec(w_main.shape, const), pl.BlockSpec(w_tail.shape, const),
        pos_spec, pos_spec,
        pl.BlockSpec((1, seg_w), const), pl.BlockSpec((1, seg_w), const), pl.BlockSpec((1, LANES), const),
        pl.BlockSpec(gmat.shape, const),
    ]
    seg_spec = pl.BlockSpec((tm, seg_w), row)
    out_dtypes = [act_dtype, act_dtype, act_dtype, act_dtype, act_dtype, F32, BF16, F32, BF16, act_dtype]
    out_shape = [jax.ShapeDtypeStruct((n, seg_w), dt) for dt in out_dtypes]
    out_shape.append(jax.ShapeDtypeStruct((n, LANES), F32))
    out_specs = [seg_spec] * 10 + [pl.BlockSpec((tm, LANES), row)]
    body = functools.partial(_inproj_body, seg_w=seg_w, d_idx=d_idx, iw_scale=H_IDX ** -0.5,
                             k_scale=d_ret ** -0.5, q_scale=d_head ** -0.5, precise=precise)
    return pl.pallas_call(
        body, grid=grid, in_specs=in_specs, out_specs=out_specs, out_shape=out_shape,
        compiler_params=_cparams(("arbitrary",)), name="inproj",
    )(x2d, sc, sh, n1, w_main, w_tail, cos, sin, qn, kn, ikn, gmat)


def _group_norm_gate(o, g, gnw):
    mu = jnp.mean(o, axis=-1, keepdims=True)
    var = jnp.mean((o - mu) ** 2, axis=-1, keepdims=True)
    on = (o - mu) * lax.rsqrt(var + EPS) * gnw
    g = g.astype(F32)
    return g * _sigmoid(g) * on


def _ret_prompt_body(lg_ref, q_ref, kt_ref, v_ref, g_ref, gnw_ref, o_ref, s_ref, *, chunk):
    t, dk = q_ref.shape
    lg = lg_ref[pl.program_id(1)]
    ii = lax.broadcasted_iota(I32, (chunk, chunk), 0)
    jj = lax.broadcasted_iota(I32, (chunk, chunk), 1)
    rel = (ii - jj).astype(F32)
    decay = jnp.where(rel >= 0.0, jnp.exp(jnp.maximum(rel, 0.0) * lg), 0.0)
    q_dec = jnp.exp((lax.broadcasted_iota(I32, (chunk, 1), 0).astype(F32) + 1.0) * lg)
    k_dec = jnp.exp((chunk - 1.0 - lax.broadcasted_iota(I32, (1, chunk), 1).astype(F32)) * lg)
    g_chunk = jnp.exp(jnp.full((1, 1), chunk, F32) * lg)
    gnw = gnw_ref[...]
    state = jnp.zeros((dk, v_ref.shape[1]), F32)
    for c in range(t // chunk):
        rows = slice(c * chunk, (c + 1) * chunk)
        qc = q_ref[rows, :]
        ktc = kt_ref[:, rows]
        vc = v_ref[rows, :]
        inner = _dot(qc, ktc) * decay
        o = _dot(inner.astype(BF16), vc) + _dot(qc, state.astype(BF16)) * q_dec
        state = state * g_chunk + _dot((ktc.astype(F32) * k_dec).astype(BF16), vc)
        o_ref[rows, :] = _group_norm_gate(o, g_ref[rows, :], gnw).astype(o_ref.dtype)
    s_ref[...] = state


def _ret_prompt(log_g, q, kt, v, g, gnw, *, chunk):
    b, h, t, dk = q.shape
    dv = v.shape[-1]
    blk = lambda r, c: pl.BlockSpec((None, None, r, c), lambda i, j, lg: (i, j, 0, 0))
    grid_spec = pltpu.PrefetchScalarGridSpec(
        num_scalar_prefetch=1, grid=(b, h),
        in_specs=[blk(t, dk), blk(dk, t), blk(t, dv), blk(t, dv),
                  pl.BlockSpec((None, 1, dv), lambda i, j, lg: (j, 0, 0))],
        out_specs=[blk(t, dv), blk(dk, dv)],
    )
    return pl.pallas_call(
        functools.partial(_ret_prompt_body, chunk=chunk), grid_spec=grid_spec,
        out_shape=[jax.ShapeDtypeStruct((b, h, t, dv), BF16), jax.ShapeDtypeStruct((b, h, dk, dv), F32)],
        compiler_params=_cparams(("arbitrary", "arbitrary")), name="ret_prompt",
    )(log_g, q, kt, v, g, gnw)


def _ret_sample_body(lg_ref, qc_ref, kc_ref, v_ref, g_ref, s0_ref, gnw_ref, o_ref, s_ref):
    for h in range(qc_ref.shape[0]):
        gamma = jnp.exp(jnp.full((1, 1), 1.0, F32) * lg_ref[h])
        qc = qc_ref[h]
        kc = kc_ref[h]
        v = v_ref[h]
        s0 = s0_ref[h]
        qk = jnp.sum(qc * kc, axis=0, keepdims=True)
        o = qk * v + jnp.sum(qc * s0, axis=0, keepdims=True) * gamma
        s_ref[h] = s0 * gamma + kc * v
        o_ref[h] = _group_norm_gate(o, g_ref[h], gnw_ref[h]).astype(o_ref.dtype)


def _ret_sample(log_g, q, k, v, g, s0, gnw):
    db, h, dk = q.shape
    dv = v.shape[-1]
    col = pl.BlockSpec((None, h, dk, 1), lambda i, lg: (i, 0, 0, 0))
    rowv = pl.BlockSpec((None, h, 1, dv), lambda i, lg: (i, 0, 0, 0))
    mat = pl.BlockSpec((None, h, dk, dv), lambda i, lg: (i, 0, 0, 0))
    grid_spec = pltpu.PrefetchScalarGridSpec(
        num_scalar_prefetch=1, grid=(db,),
        in_specs=[col, col, rowv, rowv, mat, pl.BlockSpec((h, 1, dv), lambda i, lg: (0, 0, 0))],
        out_specs=[rowv, mat],
    )
    o, s = pl.pallas_call(
        _ret_sample_body, grid_spec=grid_spec,
        out_shape=[jax.ShapeDtypeStruct((db, h, 1, dv), F32), jax.ShapeDtypeStruct((db, h, dk, dv), F32)],
        compiler_params=_cparams(("arbitrary",)), name="ret_sample",
    )(log_g, q.reshape(db, h, dk, 1), k.reshape(db, h, dk, 1), v.reshape(db, h, 1, dv),
      g.reshape(db, h, 1, dv), s0, gnw)
    return o.reshape(db, h * dv), s


ROW_CHAINS = 64


def _reduce_rows(x, combine, finish):
    rows = x.shape[0]
    width = math.gcd(rows, ROW_CHAINS)
    acc = x[0:width]
    for c in range(1, rows // width):
        acc = combine(acc, x[c * width:(c + 1) * width])
    return finish(acc, axis=0, keepdims=True)


def _select_topk(sc_ref, pos, n_visible, k_top, n_pos_bits, axis):
    def count(mask):
        ones = jnp.where(mask, 1.0, 0.0)
        return _reduce_rows(ones, jnp.add, jnp.sum) if axis == 0 else jnp.sum(ones, axis=axis, keepdims=True)

    kf = float(k_top)
    c0 = count(sc_ref[...] >= 0.0)
    cur = jnp.where(c0 >= kf, jnp.int32(0), jnp.int32(INT_MIN))

    def value_step(i, cur):
        cand = cur + (jnp.int32(1) << (30 - i))
        return jnp.where(count(sc_ref[...] >= _ordered_float(cand)) >= kf, cand, cur)

    thr = _ordered_float(lax.fori_loop(0, 31, value_step, cur))
    need = kf - count(sc_ref[...] > thr)

    def tie_step(i, m):
        cand = m + (jnp.int32(1) << (n_pos_bits - 1 - i))
        below = count((sc_ref[...] == thr) & (pos < cand))
        return jnp.where(below < need, cand, m)

    cut = lax.fori_loop(0, n_pos_bits, tie_step, jnp.zeros(thr.shape, I32))
    sc = sc_ref[...]
    picked = (sc > thr) | ((sc == thr) & (pos <= cut))
    take_all = jnp.broadcast_to(n_visible, sc.shape) <= k_top
    return (take_all & (sc > NEG_INF)) | (~take_all & picked)


def _dsa_prompt_body(ik_ref, iqt_ref, iwt_ref, aqt_ref, k_ref, vt_ref, ot_ref, sc_ref, bias_ref,
                     *, k_top, d_idx, d_head, first_qblock):
    t, qb = sc_ref.shape
    j = pl.program_id(1) + first_qblock
    krow = lax.broadcasted_iota(I32, (t, qb), 0)
    qpos = j * qb + lax.broadcasted_iota(I32, (1, qb), 1)

    ik = ik_ref[...]
    sc = jnp.zeros((t, qb), F32)
    for h in range(iwt_ref.shape[0]):
        s = _dot(ik, iqt_ref[h * d_idx:(h + 1) * d_idx, :])
        sc = sc + jnp.maximum(s, 0.0) * iwt_ref[h:h + 1, :]
    sc_ref[...] = jnp.where(krow <= qpos, sc, NEG_INF)
    sel = _select_topk(sc_ref, krow, qpos + 1, k_top, int(math.ceil(math.log2(t))), 0)
    bias_ref[...] = jnp.where(sel, 0.0, NEG_INF)

    for h in range(k_ref.shape[0]):
        rows = slice(h * d_head, (h + 1) * d_head)
        s = _dot(k_ref[h], aqt_ref[rows, :]) + bias_ref[...]
        m = _reduce_rows(s, jnp.maximum, jnp.max)
        p = jnp.exp(s - m)
        l = _reduce_rows(p, jnp.add, jnp.sum)
        o = _dot(vt_ref[rows, :], p.astype(BF16))
        ot_ref[rows, :] = (o / l).astype(ot_ref.dtype)


def _dsa_prompt(ik, iqt, iwt, aqt, kh, vt, *, qb, k_top, n_splits):
    b, t, d_idx = ik.shape
    n_heads, d_head = kh.shape[1], kh.shape[3]
    w = vt.shape[1]
    nq = t // qb // n_splits
    outs = []
    for s in range(n_splits):
        tk = (s + 1) * t // n_splits
        per_b = lambda r, c: pl.BlockSpec((None, r, c), lambda i, j: (i, 0, 0))
        per_q = lambda r, s=s: pl.BlockSpec((None, r, qb), lambda i, j: (i, 0, j + s * nq))
        outs.append(pl.pallas_call(
            functools.partial(_dsa_prompt_body, k_top=k_top, d_idx=d_idx, d_head=d_head, first_qblock=s * nq),
            grid=(b, nq),
            in_specs=[per_b(tk, d_idx), per_q(iqt.shape[1]), per_q(iwt.shape[1]), per_q(w),
                      pl.BlockSpec((None, n_heads, tk, d_head), lambda i, j: (i, 0, 0, 0)), per_b(w, tk)],
            out_specs=pl.BlockSpec((None, w, qb), lambda i, j: (i, 0, j)),
            out_shape=jax.ShapeDtypeStruct((b, w, nq * qb), BF16),
            scratch_shapes=[pltpu.VMEM((tk, qb), F32), pltpu.VMEM((tk, qb), F32)],
            compiler_params=_cparams(("arbitrary", "arbitrary")), name=f"dsa_prompt_{s}",
        )(ik, iqt, iwt, aqt, kh, vt))
    return jnp.concatenate(outs, axis=2)


def _sidx_body(pt_ref, qi_ref, iw_ref, iknew_ref, cache_ref, out_ref, buf_ref, sem_ref, *, n_pages):
    b = pl.program_id(0)
    nb = pl.num_programs(0)
    slot = b % 2
    past = n_pages * PAGE_SIZE

    def copies(seq, s):
        return [pltpu.make_async_copy(cache_ref.at[pt_ref[seq, p]],
                                      buf_ref.at[s, :, pl.ds(p * PAGE_SIZE, PAGE_SIZE)], sem_ref.at[s])
                for p in range(n_pages)]

    @pl.when(b == 0)
    def _():
        for c in copies(0, 0):
            c.start()

    @pl.when(b + 1 < nb)
    def _():
        for c in copies(b + 1, 1 - slot):
            c.start()

    for c in copies(b, slot):
        c.wait()
    lane = lax.broadcasted_iota(I32, (buf_ref.shape[1], LANES), 1)
    buf_ref[slot, :, past:] = jnp.where(lane == 0, iknew_ref[...], 0.0)

    s = _dot_hp(qi_ref[...], buf_ref[slot])
    out_ref[...] = jnp.sum(jnp.maximum(s, 0.0) * iw_ref[...], axis=0, keepdims=True)


def _sample_index_scores(page_table, qi, iw, ik_new, cache_t):
    db, n_pages = page_table.shape
    d_idx = cache_t.shape[1]
    n_keys = n_pages * PAGE_SIZE + LANES
    grid_spec = pltpu.PrefetchScalarGridSpec(
        num_scalar_prefetch=1, grid=(db,),
        in_specs=[pl.BlockSpec((None, H_IDX, d_idx), lambda i, pt: (i, 0, 0)),
                  pl.BlockSpec((None, H_IDX, 1), lambda i, pt: (i, 0, 0)),
                  pl.BlockSpec((None, d_idx, 1), lambda i, pt: (i, 0, 0)),
                  pl.BlockSpec(memory_space=pl.ANY)],
        out_specs=pl.BlockSpec((None, 1, n_keys), lambda i, pt: (i, 0, 0)),
        scratch_shapes=[pltpu.VMEM((2, d_idx, n_keys), F32), pltpu.SemaphoreType.DMA((2,))],
    )
    out = pl.pallas_call(
        functools.partial(_sidx_body, n_pages=n_pages), grid_spec=grid_spec,
        out_shape=jax.ShapeDtypeStruct((db, 1, n_keys), F32),
        compiler_params=_cparams(("arbitrary",)), name="sample_index_scores",
    )(page_table, qi, iw, ik_new, cache_t)
    return out.reshape(db, n_keys)


def _ssel_body(sc_ref, bias_ref, work_ref, *, n_visible, k_top):
    n, w = sc_ref.shape
    pos = lax.broadcasted_iota(I32, (n, w), 1)
    work_ref[...] = jnp.where(pos < n_visible, sc_ref[...], NEG_INF)
    sel = _select_topk(work_ref, pos, jnp.int32(n_visible), k_top, int(math.ceil(math.log2(w))), 1)
    bias_ref[...] = jnp.where(sel, 0.0, NEG_INF)


def _sample_select(scores, *, n_visible, k_top):
    return pl.pallas_call(
        functools.partial(_ssel_body, n_visible=n_visible, k_top=k_top),
        out_shape=jax.ShapeDtypeStruct(scores.shape, F32),
        scratch_shapes=[pltpu.VMEM(scores.shape, F32)],
        compiler_params=pltpu.CompilerParams(vmem_limit_bytes=VMEM_LIMIT), name="sample_select",
    )(scores)


def _sattn_body(pt_ref, q_ref, bias_ref, biasnew_ref, knew_ref, vnew_ref, ck_ref, cv_ref, o_ref,
                kbuf_ref, vbuf_ref, ksem_ref, vsem_ref, m_ref, l_ref, acc_ref, *, group):
    b = pl.program_id(0)
    g = pl.program_id(1)
    ng = pl.num_programs(1)
    step = b * ng + g
    total = pl.num_programs(0) * ng
    slot = step % 2
    n_heads, d_head, _ = q_ref.shape

    def copies(seq, grp, s):
        out = []
        for p in range(group):
            page = pt_ref[seq, grp * group + p]
            out.append(pltpu.make_async_copy(ck_ref.at[page], kbuf_ref.at[s, p], ksem_ref.at[s]))
            out.append(pltpu.make_async_copy(cv_ref.at[page], vbuf_ref.at[s, p], vsem_ref.at[s]))
        return out

    @pl.when(step == 0)
    def _():
        for c in copies(0, 0, 0):
            c.start()

    @pl.when(step + 1 < total)
    def _():
        nxt = step + 1
        for c in copies(nxt // ng, nxt % ng, 1 - slot):
            c.start()

    for c in copies(b, g, slot):
        c.wait()

    @pl.when(g == 0)
    def _():
        m_ref[...] = jnp.full(m_ref.shape, NEG_INF, F32)
        l_ref[...] = jnp.zeros(l_ref.shape, F32)
        acc_ref[...] = jnp.zeros(acc_ref.shape, F32)

    def online_update(h, scores, values):
        top = scores[0]
        for s in scores[1:]:
            top = jnp.maximum(top, s)
        m_old = m_ref[h]
        m_new = jnp.maximum(m_old, jnp.max(top, axis=-1, keepdims=True))
        m_safe = jnp.where(m_new == NEG_INF, 0.0, m_new)
        alpha = jnp.exp(m_old - m_safe)
        acc = alpha * acc_ref[h]
        l = alpha * l_ref[h]
        for s, value in zip(scores, values):
            p = jnp.exp(s - m_safe)
            l = l + jnp.sum(p, axis=-1, keepdims=True)
            acc = acc + value(p)
        m_ref[h] = m_new
        l_ref[h] = l
        acc_ref[h] = acc

    for h in range(n_heads):
        q = q_ref[h]
        scores = [jnp.sum(kbuf_ref[slot, p, h] * q, axis=0, keepdims=True) + bias_ref[p:p + 1, :]
                  for p in range(group)]
        values = [lambda pr, p=p, h=h: pr * vbuf_ref[slot, p, h] for p in range(group)]
        online_update(h, scores, values)

    @pl.when(g == ng - 1)
    def _():
        first_lane = lax.broadcasted_iota(I32, (d_head, LANES), 1) == 0
        for h in range(n_heads):
            s_new = jnp.sum(q_ref[h] * knew_ref[h], axis=0, keepdims=True) + biasnew_ref[:, 0:1]
            online_update(h, [s_new], [lambda pr, h=h: jnp.where(first_lane, pr * vnew_ref[h], 0.0)])
            o_ref[h] = jnp.sum(acc_ref[h], axis=-1, keepdims=True) / l_ref[h]


def _sample_attention(page_table, q, bias, k_new, v_new, cache_kt, cache_vt, *, group):
    db, n_pages = page_table.shape
    _, n_heads, d_head, _ = q.shape
    ng = n_pages // group
    past = n_pages * PAGE_SIZE
    col = pl.BlockSpec((None, n_heads, d_head, 1), lambda i, j, pt: (i, 0, 0, 0))
    any_spec = pl.BlockSpec(memory_space=pl.ANY)
    grid_spec = pltpu.PrefetchScalarGridSpec(
        num_scalar_prefetch=1, grid=(db, ng),
        in_specs=[col,
                  pl.BlockSpec((None, None, group, PAGE_SIZE), lambda i, j, pt: (i, j, 0, 0)),
                  pl.BlockSpec((None, 1, LANES), lambda i, j, pt: (i, 0, 0)),
                  col, col, any_spec, any_spec],
        out_specs=col,
        scratch_shapes=[pltpu.VMEM((2, group, n_heads, d_head, PAGE_SIZE), F32),
                        pltpu.VMEM((2, group, n_heads, d_head, PAGE_SIZE), F32),
                        pltpu.SemaphoreType.DMA((2,)), pltpu.SemaphoreType.DMA((2,)),
                        pltpu.VMEM((n_heads, 1, 1), F32), pltpu.VMEM((n_heads, 1, 1), F32),
                        pltpu.VMEM((n_heads, d_head, LANES), F32)],
    )
    return pl.pallas_call(
        functools.partial(_sattn_body, group=group), grid_spec=grid_spec,
        out_shape=jax.ShapeDtypeStruct((db, n_heads, d_head, 1), F32),
        compiler_params=_cparams(("arbitrary", "arbitrary")), name="sample_attention",
    )(page_table, q, bias[:, :past].reshape(db, ng, group, PAGE_SIZE), bias[:, past:].reshape(db, 1, LANES),
      k_new, v_new, cache_kt, cache_vt)


def _outproj_body(x_ref, ret_ref, att_ref, wo_ref, g1_ref, sc2_ref, sh2_ref, n2_ref, rw_ref,
                  rb_ref, cnt_in_ref, x1_ref, h2_ref, route_ref, cnt_out_ref, cnt_ref, *, n_experts, precise):
    half = ret_ref.shape[1]
    mm = _dot_hp if precise else _dot
    y = mm(ret_ref[...], wo_ref[:half, :]) + mm(att_ref[...], wo_ref[half:, :])
    x1 = x_ref[...] + g1_ref[...] * y
    x1_ref[...] = x1
    ms = jnp.mean(x1 * x1, axis=-1, keepdims=True)
    h2 = x1 * lax.rsqrt(ms + EPS) * n2_ref[...] * (1.0 + sc2_ref[...]) + sh2_ref[...]
    h2_ref[...] = h2
    hi, lo = _split_bf16(h2)
    rw_hi, rw_lo = _split_bf16(rw_ref[...])
    logits = _dot(hi, rw_hi) + _dot(lo, rw_hi) + _dot(hi, rw_lo) + rb_ref[...]

    tm = logits.shape[0]
    lane = lax.broadcasted_iota(I32, (tm, LANES), 1).astype(F32)
    work = jnp.where(lane < n_experts, logits, NEG_INF)
    top_vals, top_ids, top_hot = [], [], []
    for _ in range(TOP_K):
        m = jnp.max(work, axis=-1, keepdims=True)
        first = jnp.min(jnp.where(work == m, lane, float(LANES)), axis=-1, keepdims=True)
        hot = lane == first
        top_vals.append(m)
        top_ids.append(first)
        top_hot.append(hot)
        work = jnp.where(hot, NEG_INF, work)
    exps = [jnp.exp(v - top_vals[0]) for v in top_vals]
    den = exps[0]
    for e in exps[1:]:
        den = den + e

    @pl.when(pl.program_id(0) == 0)
    def _():
        cnt_ref[...] = cnt_in_ref[...]

    picks = jnp.zeros((tm, LANES), F32)
    for hot in top_hot:
        picks = picks + jnp.where(hot, 1.0, 0.0)
    earlier = lax.broadcasted_iota(I32, (tm, tm), 1) < lax.broadcasted_iota(I32, (tm, tm), 0)
    before = _dot(jnp.where(earlier, 1.0, 0.0).astype(BF16), picks.astype(BF16)) + cnt_ref[...]
    cnt_ref[...] += jnp.sum(picks, axis=0, keepdims=True)
    cnt_out_ref[...] = cnt_ref[...]

    route = jnp.zeros((tm, LANES), F32)
    for k in range(TOP_K):
        rank = jnp.sum(jnp.where(top_hot[k], before, 0.0), axis=-1, keepdims=True)
        route = jnp.where(lane == k, top_ids[k], route)
        route = jnp.where(lane == TOP_K + k, exps[k] / den, route)
        route = jnp.where(lane == 2 * TOP_K + k, rank, route)
    route_ref[...] = route


def _outproj(x2d, ret, att, wo, g1, sc2, sh2, n2, rw, rb, cnt_in, *, tm, rows_per_batch, per_row,
             n_experts, precise):
    n, d = x2d.shape
    half = ret.shape[1]
    row = lambda i: (i, 0)
    const = lambda i: (0, 0)
    if per_row:
        mod_spec = pl.BlockSpec((tm, d), row)
    else:
        tiles_per_batch = rows_per_batch // tm
        mod_spec = pl.BlockSpec((None, 1, d), lambda i: (i // tiles_per_batch, 0, 0))
    return pl.pallas_call(
        functools.partial(_outproj_body, n_experts=n_experts, precise=precise), grid=(n // tm,),
        in_specs=[pl.BlockSpec((tm, d), row), pl.BlockSpec((tm, half), row), pl.BlockSpec((tm, half), row),
                  pl.BlockSpec(wo.shape, const), mod_spec, mod_spec, mod_spec, pl.BlockSpec((1, d), const),
                  pl.BlockSpec(rw.shape, const), pl.BlockSpec((1, LANES), const),
                  pl.BlockSpec((1, LANES), const)],
        out_specs=[pl.BlockSpec((tm, d), row), pl.BlockSpec((tm, d), row), pl.BlockSpec((tm, LANES), row),
                   pl.BlockSpec((1, LANES), const)],
        out_shape=[jax.ShapeDtypeStruct((n, d), F32), jax.ShapeDtypeStruct((n, d), F32),
                   jax.ShapeDtypeStruct((n, LANES), F32), jax.ShapeDtypeStruct((1, LANES), F32)],
        scratch_shapes=[pltpu.VMEM((1, LANES), F32)],
        compiler_params=_cparams(("arbitrary",)), name="outproj",
    )(x2d, ret, att, wo, g1, sc2, sh2, n2, rw, rb, cnt_in)


EXPERT_TILE = 256


def _experts_body(te_ref, src_ref, nused_ref, h_ref, wgu_ref, bgu_ref, wd_ref, bd_ref, y_ref,
                  xbuf_ref, sem_ref, wgu_bf_ref, wd_bf_ref, *, f_chunk):
    t = pl.program_id(0)
    n_tiles = pl.num_programs(0)
    slot = t % 2
    n_used = nused_ref[0]
    tm = y_ref.shape[0]
    d_ff = wd_ref.shape[0]
    n_chunks = d_ff // f_chunk

    def row_copy(token, s, r):
        return pltpu.make_async_copy(h_ref.at[token], xbuf_ref.at[s, r], sem_ref.at[s])

    def start_rows(tile, s, rows):
        for r in rows:
            row_copy(src_ref[tile * tm + r], s, r).start()

    def wait_rows(s):
        for r in range(tm):
            row_copy(0, s, r).wait()

    @pl.when((t == 0) & (n_used > 0))
    def _():
        start_rows(0, 0, range(tm))

    @pl.when((t == n_used) & (t > 0))
    def _():
        wait_rows(slot)

    @pl.when(t >= n_used)
    def _():
        y_ref[...] = jnp.zeros(y_ref.shape, F32)

    @pl.when(t < n_used)
    def _():
        wait_rows(slot)

        @pl.when((t == 0) | (te_ref[t] != te_ref[jnp.maximum(t - 1, 0)]))
        def _():
            wgu_bf_ref[...] = wgu_ref[...].astype(BF16)
            wd_bf_ref[...] = wd_ref[...].astype(BF16)

        nxt = jnp.minimum(t + 1, n_tiles - 1)
        x = xbuf_ref[slot].astype(BF16)
        y = jnp.zeros(y_ref.shape, F32)
        for j in range(n_chunks):
            cols = slice(j * f_chunk, (j + 1) * f_chunk)
            ucols = slice(d_ff + j * f_chunk, d_ff + (j + 1) * f_chunk)
            hg = _dot(x, wgu_bf_ref[:, cols]) + bgu_ref[:, cols]
            hu = _dot(x, wgu_bf_ref[:, ucols]) + bgu_ref[:, ucols]
            gate = jnp.minimum(hg, SWIGLU_LIMIT)
            up = jnp.clip(hu, -SWIGLU_LIMIT, SWIGLU_LIMIT)
            act = (up + 1.0) * gate * _sigmoid(SWIGLU_ALPHA * gate)
            y = y + _dot(act.astype(BF16), wd_bf_ref[cols, :])
            start_rows(nxt, 1 - slot, range(j * tm // n_chunks, (j + 1) * tm // n_chunks))
        y_ref[...] = y + bd_ref[...]

        @pl.when(t == n_tiles - 1)
        def _():
            wait_rows(1 - slot)


def _experts(tile_expert, src_token, n_used, h2, wgu, bgu, wd, bd, *, f_chunk):
    n_tiles = tile_expert.shape[0]
    d = h2.shape[1]
    _, _, two_f = wgu.shape
    d_ff = two_f // 2
    by_expert = lambda r, c: pl.BlockSpec((None, r, c), lambda t, te, src, nu: (te[t], 0, 0))
    grid_spec = pltpu.PrefetchScalarGridSpec(
        num_scalar_prefetch=3, grid=(n_tiles,),
        in_specs=[pl.BlockSpec(memory_space=pl.ANY), by_expert(d, two_f), by_expert(1, two_f),
                  by_expert(d_ff, d), by_expert(1, d)],
        out_specs=pl.BlockSpec((EXPERT_TILE, d), lambda t, te, src, nu: (t, 0)),
        scratch_shapes=[pltpu.VMEM((2, EXPERT_TILE, d), F32), pltpu.SemaphoreType.DMA((2,)),
                        pltpu.VMEM((d, two_f), BF16), pltpu.VMEM((d_ff, d), BF16)],
    )
    return pl.pallas_call(
        functools.partial(_experts_body, f_chunk=f_chunk), grid_spec=grid_spec,
        out_shape=jax.ShapeDtypeStruct((n_tiles * EXPERT_TILE, d), F32),
        compiler_params=_cparams(("arbitrary",)), name="experts",
    )(tile_expert, src_token, n_used, h2, wgu, bgu, wd, bd)


def _combine_body(pos_ref, y_ref, route_ref, x1_ref, g2_ref, out_ref, ybuf_ref, sem_ref, *, first_token, n_tokens):
    i = pl.program_id(0)
    slot = i % 2
    tc = out_ref.shape[0]

    def row_copy(pos, s, r, k):
        return pltpu.make_async_copy(y_ref.at[pos], ybuf_ref.at[s, k, r], sem_ref.at[s])

    def gather(tile, s):
        for k in range(TOP_K):
            for r in range(tc):
                row_copy(pos_ref[k * n_tokens + first_token + tile * tc + r], s, r, k).start()

    @pl.when(i == 0)
    def _():
        gather(0, 0)

    @pl.when(i + 1 < pl.num_programs(0))
    def _():
        gather(i + 1, 1 - slot)

    for k in range(TOP_K):
        for r in range(tc):
            row_copy(0, slot, r, k).wait()

    ff = jnp.zeros(out_ref.shape, F32)
    for k in range(TOP_K):
        ff = ff + route_ref[:, TOP_K + k:TOP_K + k + 1] * ybuf_ref[slot, k]
    out_ref[...] = x1_ref[...] + g2_ref[...] * ff


def _combine(pos_flat, y, route, x1, g2, *, tc, rows_per_batch, per_row, first_token, n_tokens):
    n, d = x1.shape
    row = lambda i, pos: (i, 0)
    if per_row:
        mod_spec = pl.BlockSpec((tc, d), row)
    else:
        tiles_per_batch = rows_per_batch // tc
        mod_spec = pl.BlockSpec((None, 1, d), lambda i, pos: (i // tiles_per_batch, 0, 0))
    grid_spec = pltpu.PrefetchScalarGridSpec(
        num_scalar_prefetch=1, grid=(n // tc,),
        in_specs=[pl.BlockSpec(memory_space=pl.ANY), pl.BlockSpec((tc, LANES), row), pl.BlockSpec((tc, d), row),
                  mod_spec],
        out_specs=pl.BlockSpec((tc, d), row),
        scratch_shapes=[pltpu.VMEM((2, TOP_K, tc, d), F32), pltpu.SemaphoreType.DMA((2,))],
    )
    return pl.pallas_call(
        functools.partial(_combine_body, first_token=first_token, n_tokens=n_tokens), grid_spec=grid_spec,
        out_shape=jax.ShapeDtypeStruct((n, d), F32),
        compiler_params=_cparams(("arbitrary",)), name="combine",
    )(pos_flat, y, route, x1, g2)


def _dispatch_plan(route, counts, n_experts, n_tiles):
    n = route.shape[0]
    ids = route[:, :TOP_K].astype(I32)
    ranks = route[:, 2 * TOP_K:3 * TOP_K].astype(I32)
    cnt = counts[0, :n_experts].astype(I32)
    tiles_per_expert = (cnt + EXPERT_TILE - 1) // EXPERT_TILE
    tile_end = jnp.cumsum(tiles_per_expert)
    row_start = (tile_end - tiles_per_expert) * EXPERT_TILE
    pos = (row_start[ids] + ranks).T
    n_used = tile_end[-1]
    tile = jnp.arange(n_tiles, dtype=I32)
    last = jnp.minimum(tile, n_used - 1)
    tile_expert = jnp.sum((tile_end[None, :] <= last[:, None]).astype(I32), axis=1)
    token = jnp.broadcast_to(jnp.arange(n, dtype=I32)[None, :], pos.shape)
    src_token = jnp.zeros((n_tiles * EXPERT_TILE,), I32).at[pos.reshape(-1)].set(token.reshape(-1))
    return pos.reshape(-1), tile_expert, src_token, n_used.reshape(1).astype(I32)


def _rope_tables(pos, d):
    pos = np.asarray(pos, np.float64)
    inv = np.power(ROPE_THETA, -np.arange(0, d, 2, dtype=np.float64) / d)
    ang = pos[:, None] * inv[None, :]
    cos = np.cos(np.concatenate([ang, ang], axis=-1))
    sin = np.sin(ang)
    sin = np.concatenate([-sin, sin], axis=-1)
    reps = LANES // d
    return jnp.asarray(np.tile(cos, (1, reps)), F32), jnp.asarray(np.tile(sin, (1, reps)), F32)


def _split_bf16(x):
    hi = x.astype(BF16)
    return hi, (x - hi.astype(F32)).astype(BF16)


def _adaln_body(c_ref, w_ref, b_ref, o_ref):
    c = c_ref[...]
    a_hi, a_lo = _split_bf16(c * _sigmoid(c))
    w_hi, w_lo = _split_bf16(w_ref[...])
    o_ref[...] = _dot(a_hi, w_hi) + _dot(a_lo, w_hi) + _dot(a_hi, w_lo) + b_ref[...]


def _adaln(c, ada_w, ada_b):
    n, d = c.shape
    n_out = ada_w.shape[1]
    tn = d
    out = pl.pallas_call(
        _adaln_body, grid=(n_out // tn,),
        in_specs=[pl.BlockSpec((n, d), lambda j: (0, 0)), pl.BlockSpec((d, tn), lambda j: (0, j)),
                  pl.BlockSpec((1, tn), lambda j: (0, j))],
        out_specs=pl.BlockSpec((n, tn), lambda j: (0, j)),
        out_shape=jax.ShapeDtypeStruct((n, n_out), F32),
        compiler_params=_cparams(("arbitrary",)), name="adaln",
    )(c, ada_w, ada_b[None])
    return jnp.split(out, 6, axis=-1)


def kernel(x_prompt, x_sample, cache_k, cache_v, cache_idx_k, state_ret, page_table, c_prompt, c_sample,
           ada_w, ada_b, norm1_w, w_in, q_norm_w, k_norm_w, idx_k_norm_w, ret_gn_w, w_out, norm2_w,
           router_w, router_b, w_gate_up, b_gate_up, w_down, b_down):
    depth = w_in.shape[0]
    b, t, d = x_prompt.shape
    db, ts, _ = x_sample.shape
    assert ts == 1, "the sample group decodes one token per sequence"
    _, _, h_ret, d_ret, _ = state_ret.shape
    _, n_pool, page, h_att, d_head = cache_k.shape
    assert page == PAGE_SIZE
    d_idx = cache_idx_k.shape[-1]
    ret_w, att_w = h_ret * d_ret, h_att * d_head
    assert ret_w == att_w == H_IDX * d_idx and d_ret == d_head == d_idx == 64
    n_experts = router_w.shape[-1]
    n_pages = page_table.shape[1]
    past = n_pages * PAGE_SIZE
    n_main = 8 * ret_w

    cos_p, sin_p = _rope_tables(np.arange(t), d_ret)
    cos_s, sin_s = _rope_tables(np.full((db,), past), d_ret)
    log_g = jnp.log1p(-jnp.power(2.0, -5.0 - jnp.arange(h_ret, dtype=F32)))
    gmat = jnp.asarray(np.kron(np.eye(ret_w // d_head), np.full((d_head, d_head), 1.0 / d_head)), BF16)

    hp, hs = x_prompt.reshape(b * t, d), x_sample.reshape(db, d)
    outs = [[] for _ in range(8)]
    for l in range(depth):
        w_main = w_in[l, :, :n_main]
        w_tail = jnp.pad(w_in[l, :, n_main:], ((0, 0), (0, LANES - (w_in.shape[2] - n_main))))
        qn = jnp.tile(q_norm_w[l], h_att)[None]
        kn = jnp.tile(k_norm_w[l], h_att)[None]
        ikn = jnp.pad(idx_k_norm_w[l], (0, LANES - d_idx))[None]
        gnw = ret_gn_w[l].reshape(h_ret, 1, d_ret)
        wo = w_out[l]
        rw = jnp.pad(router_w[l], ((0, 0), (0, LANES - n_experts)))
        rb = jnp.pad(router_b[l], (0, LANES - n_experts))[None]
        n1, n2 = norm1_w[l][None], norm2_w[l][None]
        bgu, bd = b_gate_up[l][:, None, :], b_down[l][:, None, :]
        inproj = functools.partial(_inproj, n1=n1, qn=qn, kn=kn, ikn=ikn, gmat=gmat,
                                   d_ret=d_ret, d_head=d_head, d_idx=d_idx)
        outproj = functools.partial(_outproj, n2=n2, rw=rw, rb=rb, n_experts=n_experts)

        mods = _adaln(jnp.concatenate([c_prompt, c_sample], axis=0), ada_w[l], ada_b[l])

        sh1, sc1, g1, sh2, sc2, g2 = [a[:b, None, :] for a in mods]
        rq, rk, rv, rg, aq, ak, akb, av, avb, iq, tail = inproj(
            hp, sc1, sh1, w_main=w_main.astype(BF16), w_tail=w_tail.astype(BF16), cos=cos_p, sin=sin_p,
            tm=256, rows_per_batch=t, per_row=False, precise=False)

        heads = lambda a, nh: a.reshape(b, t, nh, -1).transpose(0, 2, 1, 3)
        ret_o, ret_s = _ret_prompt(log_g, heads(rq, h_ret), heads(rk, h_ret).swapaxes(2, 3), heads(rv, h_ret),
                                   heads(rg, h_ret), gnw, chunk=256)
        ret_o = ret_o.transpose(0, 2, 1, 3).reshape(b * t, ret_w)

        tail3 = tail.reshape(b, t, LANES)
        ikp = tail3[:, :, :d_idx]
        seq_t = lambda a: a.reshape(b, t, -1).swapaxes(1, 2)
        iwt = tail3[:, :, d_idx:d_idx + H_IDX].swapaxes(1, 2) * (d_idx ** -0.5)
        att_t = _dsa_prompt(ikp.astype(BF16), seq_t(iq), iwt, seq_t(aq), heads(akb, h_att), seq_t(avb),
                            qb=128, k_top=min(TOPK_MAX, t // 4), n_splits=4)
        att_o = att_t.swapaxes(1, 2).reshape(b * t, att_w)

        x1_p, h2_p, route_p, counts = outproj(hp, ret_o, att_o, wo=wo.astype(BF16), g1=g1, sc2=sc2, sh2=sh2,
                                              cnt_in=jnp.zeros((1, LANES), F32), tm=512, rows_per_batch=t,
                                              per_row=False, precise=False)
        g2_p = g2
        outs[0].append(ak.reshape(b, t, h_att, d_head))
        outs[1].append(av.reshape(b, t, h_att, d_head))
        outs[2].append(ikp)
        outs[3].append(ret_s)

        sh1, sc1, g1, sh2, sc2, g2 = [a[b:] for a in mods]
        rq, rk, rv, rg, aq, ak, akb, av, avb, iq, tail = inproj(
            hs, sc1, sh1, w_main=w_main, w_tail=w_tail, cos=cos_s, sin=sin_s, tm=db, rows_per_batch=1,
            per_row=True, precise=True)
        hd = lambda a: a.reshape(db, h_ret, d_ret)
        ret_o, ret_s = _ret_sample(log_g, hd(rq), hd(rk), hd(rv), hd(rg), state_ret[l], gnw)

        iks = tail[:, :d_idx]
        iw = tail[:, d_idx:d_idx + H_IDX] * (d_idx ** -0.5)
        scores = _sample_index_scores(page_table, iq.reshape(db, H_IDX, d_idx), iw[:, :, None], iks[:, :, None],
                                      cache_idx_k[l].transpose(0, 2, 1))
        bias = _sample_select(scores, n_visible=past + ts, k_top=min(TOPK_MAX, (past + ts) // 4))
        hda = lambda a: a.reshape(db, h_att, d_head, 1)
        att_o = _sample_attention(page_table, hda(aq), bias, hda(ak), hda(av), cache_k[l].transpose(0, 2, 3, 1),
                                  cache_v[l].transpose(0, 2, 3, 1), group=8)

        x1_s, h2_s, route_s, counts = outproj(hs, ret_o, att_o.reshape(db, att_w), wo=wo, g1=g1, sc2=sc2, sh2=sh2,
                                              cnt_in=counts, tm=db, rows_per_batch=1, per_row=True, precise=True)

        n_all = b * t + db
        n_tiles = -(-TOP_K * n_all // EXPERT_TILE) + n_experts
        pos, tile_expert, src_token, n_used = _dispatch_plan(
            jnp.concatenate([route_p, route_s], axis=0), counts, n_experts, n_tiles)
        y = _experts(tile_expert, src_token, n_used, jnp.concatenate([h2_p, h2_s], axis=0),
                     w_gate_up[l], bgu, w_down[l], bd, f_chunk=512)
        hp = _combine(pos, y, route_p, x1_p, g2_p, tc=128, rows_per_batch=t, per_row=False, first_token=0,
                      n_tokens=n_all)
        hs = _combine(pos, y, route_s, x1_s, g2, tc=db, rows_per_batch=1, per_row=True, first_token=b * t,
                      n_tokens=n_all)
        outs[4].append(ak.reshape(db, ts, h_att, d_head))
        outs[5].append(av.reshape(db, ts, h_att, d_head))
        outs[6].append(iks.reshape(db, ts, d_idx))
        outs[7].append(ret_s)

    k_p, v_p, ik_p, r_p, k_s, v_s, ik_s, r_s = [jnp.stack(o) for o in outs]
    return (hp.reshape(b, t, d), hs.reshape(db, ts, d), k_p, v_p, ik_p, r_p, k_s, v_s, ik_s, r_s)
```

```python
import functools
import math

import numpy as np
import jax
import jax.numpy as jnp
from jax import lax
from jax.experimental import pallas as pl
from jax.experimental.pallas import tpu as pltpu

F32 = jnp.float32
BF16 = jnp.bfloat16
I32 = jnp.int32

PAGE_SIZE = 128
H_IDX = 8
TOPK_MAX = 256
TOP_K = 4
SWIGLU_LIMIT = 7.0
SWIGLU_ALPHA = 1.702
ROPE_THETA = 10000.0
EPS = 1e-6
INT_MIN = -(2 ** 31)
NEG_INF = float("-inf")

LANES = 128
VMEM_LIMIT = 56 * 1024 * 1024


def _cparams(sem):
    return pltpu.CompilerParams(dimension_semantics=sem, vmem_limit_bytes=VMEM_LIMIT)


def _dot(a, b):
    return jnp.dot(a, b, preferred_element_type=F32)


def _dot_nt(a, b):
    return lax.dot_general(a, b, (((1,), (1,)), ((), ())), preferred_element_type=F32)


def _sigmoid(x):
    return 1.0 / (1.0 + jnp.exp(-x))


def _dot_hp(a, b):
    return jnp.dot(a, b, preferred_element_type=F32, precision=lax.Precision.HIGHEST)


def _dot_nt_hp(a, b):
    return lax.dot_general(a, b, (((1,), (1,)), ((), ())), preferred_element_type=F32,
                           precision=lax.Precision.HIGHEST)


def _ordered_float(key):
    bits = jnp.where(key < 0, key ^ jnp.int32(0x7FFFFFFF), key)
    return pltpu.bitcast(bits, F32)


def _rope_halves(v, cos, sin_signed, first_half, width):
    swapped = jnp.where(first_half, pltpu.roll(v, width - 32, 1), pltpu.roll(v, 32, 1))
    return v * cos + swapped * sin_signed


def _inproj_body(x_ref, sc_ref, sh_ref, n1_ref, w_ref, wt_ref, cos_ref, sin_ref, qn_ref, kn_ref,
                 ikn_ref, gmat_ref,
                 rq_ref, rk_ref, rv_ref, rg_ref, aq_ref, ak_ref, akb_ref, av_ref, avb_ref, iq_ref,
                 tail_ref, *, seg_w, d_idx, iw_scale, k_scale, q_scale, precise):
    x = x_ref[...]
    tm = x.shape[0]
    ms = jnp.mean(x * x, axis=-1, keepdims=True)
    xn = x * lax.rsqrt(ms + EPS) * n1_ref[...]
    h = xn * (1.0 + sc_ref[...]) + sh_ref[...]
    if precise:
        mm = _dot_hp
    else:
        mm = _dot
        h = h.astype(BF16)

    cos1 = cos_ref[...]
    sin1 = sin_ref[...]
    reps = seg_w // LANES
    cos = jnp.concatenate([cos1] * reps, axis=1)
    sin = jnp.concatenate([sin1] * reps, axis=1)
    lane = lax.broadcasted_iota(I32, (tm, seg_w), 1)
    first = (lane & 32) == 0
    rope = functools.partial(_rope_halves, cos=cos, sin_signed=sin, first_half=first, width=seg_w)

    def seg(j):
        return mm(h, w_ref[:, j * seg_w:(j + 1) * seg_w])

    def head_norm(v, w):
        sq = v * v
        hi = sq.astype(BF16)
        lo = (sq - hi.astype(F32)).astype(BF16)
        msq = _dot(hi, gmat_ref[...]) + _dot(lo, gmat_ref[...])
        return v * lax.rsqrt(msq + EPS) * w

    rq_ref[...] = rope(seg(0)).astype(rq_ref.dtype)
    rk_ref[...] = (rope(seg(1)) * k_scale).astype(rk_ref.dtype)
    rv_ref[...] = seg(2).astype(rv_ref.dtype)
    rg_ref[...] = seg(3).astype(rg_ref.dtype)
    aq_ref[...] = (rope(head_norm(seg(4), qn_ref[...])) * q_scale).astype(aq_ref.dtype)
    ak = rope(head_norm(seg(5), kn_ref[...]))
    ak_ref[...] = ak
    akb_ref[...] = ak.astype(BF16)
    av = seg(6)
    av_ref[...] = av
    avb_ref[...] = av.astype(BF16)
    iq_ref[...] = rope(seg(7)).astype(iq_ref.dtype)

    zt = mm(h, wt_ref[...])
    lane_t = lax.broadcasted_iota(I32, (tm, LANES), 1)
    is_key = lane_t < d_idx
    msk = jnp.sum(jnp.where(is_key, zt * zt, 0.0), axis=-1, keepdims=True) * (1.0 / d_idx)
    kn = zt * lax.rsqrt(msk + EPS) * ikn_ref[...]
    kr = _rope_halves(kn, cos1, sin1, (lane_t & 32) == 0, LANES)
    tail_ref[...] = jnp.where(is_key, kr, zt * iw_scale)


def _inproj(x2d, sc, sh, n1, w_main, w_tail, cos, sin, qn, kn, ikn, gmat, *, tm, rows_per_batch,
            per_row, precise, d_ret, d_head, d_idx):
    act_dtype = F32 if precise else BF16
    n, d = x2d.shape
    seg_w = w_main.shape[1] // 8
    grid = (n // tm,)
    row = lambda i: (i, 0)
    const = lambda i: (0, 0)
    if per_row:
        mod_spec = pl.BlockSpec((tm, d), row)
        pos_spec = pl.BlockSpec((tm, LANES), row)
    else:
        tiles_per_batch = rows_per_batch // tm
        mod_spec = pl.BlockSpec((None, 1, d), lambda i: (i // tiles_per_batch, 0, 0))
        pos_spec = pl.BlockSpec((tm, LANES), lambda i: (i % tiles_per_batch, 0))
    in_specs = [
        pl.BlockSpec((tm, d), row), mod_spec, mod_spec, pl.BlockSpec((1, d), const),
        pl.BlockSpec(w_main.shape, const), pl.BlockSpec(w_tail.shape, const),
        pos_spec, pos_spec,
        pl.BlockSpec((1, seg_w), const), pl.BlockSpec((1, seg_w), const), pl.BlockSpec((1, LANES), const),
        pl.BlockSpec(gmat.shape, const),
    ]
    seg_spec = pl.BlockSpec((tm, seg_w), row)
    out_dtypes = [act_dtype, act_dtype, act_dtype, act_dtype, act_dtype, F32, BF16, F32, BF16, act_dtype]
    out_shape = [jax.ShapeDtypeStruct((n, seg_w), dt) for dt in out_dtypes]
    out_shape.append(jax.ShapeDtypeStruct((n, LANES), F32))
    out_specs = [seg_spec] * 10 + [pl.BlockSpec((tm, LANES), row)]
    body = functools.partial(_inproj_body, seg_w=seg_w, d_idx=d_idx, iw_scale=H_IDX ** -0.5,
                             k_scale=d_ret ** -0.5, q_scale=d_head ** -0.5, precise=precise)
    return pl.pallas_call(
        body, grid=grid, in_specs=in_specs, out_specs=out_specs, out_shape=out_shape,
        compiler_params=_cparams(("arbitrary",)), name="inproj",
    )(x2d, sc, sh, n1, w_main, w_tail, cos, sin, qn, kn, ikn, gmat)


def _group_norm_gate(o, g, gnw):
    mu = jnp.mean(o, axis=-1, keepdims=True)
    var = jnp.mean((o - mu) ** 2, axis=-1, keepdims=True)
    on = (o - mu) * lax.rsqrt(var + EPS) * gnw
    g = g.astype(F32)
    return g * _sigmoid(g) * on


def _ret_prompt_body(lg_ref, q_ref, kt_ref, v_ref, g_ref, gnw_ref, o_ref, s_ref, *, chunk):
    t, dk = q_ref.shape
    lg = lg_ref[pl.program_id(1)]
    ii = lax.broadcasted_iota(I32, (chunk, chunk), 0)
    jj = lax.broadcasted_iota(I32, (chunk, chunk), 1)
    rel = (ii - jj).astype(F32)
    decay = jnp.where(rel >= 0.0, jnp.exp(jnp.maximum(rel, 0.0) * lg), 0.0)
    q_dec = jnp.exp((lax.broadcasted_iota(I32, (chunk, 1), 0).astype(F32) + 1.0) * lg)
    k_dec = jnp.exp((chunk - 1.0 - lax.broadcasted_iota(I32, (1, chunk), 1).astype(F32)) * lg)
    g_chunk = jnp.exp(jnp.full((1, 1), chunk, F32) * lg)
    gnw = gnw_ref[...]
    state = jnp.zeros((dk, v_ref.shape[1]), F32)
    for c in range(t // chunk):
        rows = slice(c * chunk, (c + 1) * chunk)
        qc = q_ref[rows, :]
        ktc = kt_ref[:, rows]
        vc = v_ref[rows, :]
        inner = _dot(qc, ktc) * decay
        o = _dot(inner.astype(BF16), vc) + _dot(qc, state.astype(BF16)) * q_dec
        state = state * g_chunk + _dot((ktc.astype(F32) * k_dec).astype(BF16), vc)
        o_ref[rows, :] = _group_norm_gate(o, g_ref[rows, :], gnw).astype(o_ref.dtype)
    s_ref[...] = state


def _ret_prompt(log_g, q, kt, v, g, gnw, *, chunk):
    b, h, t, dk = q.shape
    dv = v.shape[-1]
    blk = lambda r, c: pl.BlockSpec((None, None, r, c), lambda i, j, lg: (i, j, 0, 0))
    grid_spec = pltpu.PrefetchScalarGridSpec(
        num_scalar_prefetch=1, grid=(b, h),
        in_specs=[blk(t, dk), blk(dk, t), blk(t, dv), blk(t, dv),
                  pl.BlockSpec((None, 1, dv), lambda i, j, lg: (j, 0, 0))],
        out_specs=[blk(t, dv), blk(dk, dv)],
    )
    return pl.pallas_call(
        functools.partial(_ret_prompt_body, chunk=chunk), grid_spec=grid_spec,
        out_shape=[jax.ShapeDtypeStruct((b, h, t, dv), BF16), jax.ShapeDtypeStruct((b, h, dk, dv), F32)],
        compiler_params=_cparams(("arbitrary", "arbitrary")), name="ret_prompt",
    )(log_g, q, kt, v, g, gnw)


def _ret_sample_body(lg_ref, qc_ref, kc_ref, v_ref, g_ref, s0_ref, gnw_ref, o_ref, s_ref):
    for h in range(qc_ref.shape[0]):
        gamma = jnp.exp(jnp.full((1, 1), 1.0, F32) * lg_ref[h])
        qc = qc_ref[h]
        kc = kc_ref[h]
        v = v_ref[h]
        s0 = s0_ref[h]
        qk = jnp.sum(qc * kc, axis=0, keepdims=True)
        o = qk * v + jnp.sum(qc * s0, axis=0, keepdims=True) * gamma
        s_ref[h] = s0 * gamma + kc * v
        o_ref[h] = _group_norm_gate(o, g_ref[h], gnw_ref[h]).astype(o_ref.dtype)


def _ret_sample(log_g, q, k, v, g, s0, gnw):
    db, h, dk = q.shape
    dv = v.shape[-1]
    col = pl.BlockSpec((None, h, dk, 1), lambda i, lg: (i, 0, 0, 0))
    rowv = pl.BlockSpec((None, h, 1, dv), lambda i, lg: (i, 0, 0, 0))
    mat = pl.BlockSpec((None, h, dk, dv), lambda i, lg: (i, 0, 0, 0))
    grid_spec = pltpu.PrefetchScalarGridSpec(
        num_scalar_prefetch=1, grid=(db,),
        in_specs=[col, col, rowv, rowv, mat, pl.BlockSpec((h, 1, dv), lambda i, lg: (0, 0, 0))],
        out_specs=[rowv, mat],
    )
    o, s = pl.pallas_call(
        _ret_sample_body, grid_spec=grid_spec,
        out_shape=[jax.ShapeDtypeStruct((db, h, 1, dv), F32), jax.ShapeDtypeStruct((db, h, dk, dv), F32)],
        compiler_params=_cparams(("arbitrary",)), name="ret_sample",
    )(log_g, q.reshape(db, h, dk, 1), k.reshape(db, h, dk, 1), v.reshape(db, h, 1, dv),
      g.reshape(db, h, 1, dv), s0, gnw)
    return o.reshape(db, h * dv), s


ROW_CHAINS = 64
KEY_CHUNK = 256


def _reduce_rows(x, combine, finish):
    rows = x.shape[0]
    width = math.gcd(rows, ROW_CHAINS)
    acc = x[0:width]
    for c in range(1, rows // width):
        acc = combine(acc, x[c * width:(c + 1) * width])
    return finish(acc, axis=0, keepdims=True)


def _select_topk(sc_ref, pos, n_visible, k_top, n_pos_bits, axis):
    def count(mask):
        ones = jnp.where(mask, 1.0, 0.0)
        return _reduce_rows(ones, jnp.add, jnp.sum) if axis == 0 else jnp.sum(ones, axis=axis, keepdims=True)

    kf = float(k_top)
    c0 = count(sc_ref[...] >= 0.0)
    cur = jnp.where(c0 >= kf, jnp.int32(0), jnp.int32(INT_MIN))

    def value_step(i, cur):
        cand = cur + (jnp.int32(1) << (30 - i))
        return jnp.where(count(sc_ref[...] >= _ordered_float(cand)) >= kf, cand, cur)

    thr = _ordered_float(lax.fori_loop(0, 31, value_step, cur))
    need = kf - count(sc_ref[...] > thr)

    def tie_step(i, m):
        cand = m + (jnp.int32(1) << (n_pos_bits - 1 - i))
        below = count((sc_ref[...] == thr) & (pos < cand))
        return jnp.where(below < need, cand, m)

    surplus = jnp.max(count(sc_ref[...] == thr) - need)
    cut = lax.cond(surplus > 0.0,
                   lambda: lax.fori_loop(0, n_pos_bits, tie_step, jnp.zeros(thr.shape, I32)),
                   lambda: jnp.full(thr.shape, (1 << n_pos_bits) - 1, I32))
    sc = sc_ref[...]
    picked = (sc > thr) | ((sc == thr) & (pos <= cut))
    take_all = jnp.broadcast_to(n_visible, sc.shape) <= k_top
    return (take_all & (sc > NEG_INF)) | (~take_all & picked)


def _dsa_prompt_body(ik_ref, iqt_ref, iwt_ref, aqt_ref, k_ref, vt_ref, ot_ref, sc_ref, bias_ref,
                     *, k_top, d_idx, d_head, first_qblock):
    t, qb = sc_ref.shape
    j = pl.program_id(1) + first_qblock
    krow = lax.broadcasted_iota(I32, (t, qb), 0)
    qpos = j * qb + lax.broadcasted_iota(I32, (1, qb), 1)

    kc = math.gcd(t, KEY_CHUNK)
    chunks = [slice(c * kc, (c + 1) * kc) for c in range(t // kc)]

    for rows in chunks:
        ik = ik_ref[rows, :]
        sc = jnp.zeros((kc, qb), F32)
        for h in range(iwt_ref.shape[0]):
            s = _dot(ik, iqt_ref[h * d_idx:(h + 1) * d_idx, :])
            sc = sc + jnp.maximum(s, 0.0) * iwt_ref[h:h + 1, :]
        sc_ref[rows, :] = jnp.where(krow[rows, :] <= qpos, sc, NEG_INF)
    sel = _select_topk(sc_ref, krow, qpos + 1, k_top, int(math.ceil(math.log2(t))), 0)
    bias_ref[...] = jnp.where(sel, 0.0, NEG_INF)

    for h in range(k_ref.shape[0]):
        hrows = slice(h * d_head, (h + 1) * d_head)
        s = _dot(k_ref[h], aqt_ref[hrows, :]) + bias_ref[...]
        m = _reduce_rows(s, jnp.maximum, jnp.max)
        p = jnp.exp(s - m)
        l = _reduce_rows(p, jnp.add, jnp.sum)
        o = _dot(vt_ref[hrows, :], p.astype(BF16))
        ot_ref[hrows, :] = (o / l).astype(ot_ref.dtype)


def _dsa_prompt(ik, iqt, iwt, aqt, kh, vt, *, qb, k_top, n_splits):
    b, t, d_idx = ik.shape
    n_heads, d_head = kh.shape[1], kh.shape[3]
    w = vt.shape[1]
    nq = t // qb // n_splits
    outs = []
    for s in range(n_splits):
        tk = (s + 1) * t // n_splits
        per_b = lambda r, c: pl.BlockSpec((None, r, c), lambda i, j: (i, 0, 0))
        per_q = lambda r, s=s: pl.BlockSpec((None, r, qb), lambda i, j: (i, 0, j + s * nq))
        outs.append(pl.pallas_call(
            functools.partial(_dsa_prompt_body, k_top=k_top, d_idx=d_idx, d_head=d_head, first_qblock=s * nq),
            grid=(b, nq),
            in_specs=[per_b(tk, d_idx), per_q(iqt.shape[1]), per_q(iwt.shape[1]), per_q(w),
                      pl.BlockSpec((None, n_heads, tk, d_head), lambda i, j: (i, 0, 0, 0)), per_b(w, tk)],
            out_specs=pl.BlockSpec((None, w, qb), lambda i, j: (i, 0, j)),
            out_shape=jax.ShapeDtypeStruct((b, w, nq * qb), BF16),
            scratch_shapes=[pltpu.VMEM((tk, qb), F32), pltpu.VMEM((tk, qb), F32)],
            compiler_params=_cparams(("arbitrary", "arbitrary")), name=f"dsa_prompt_{s}",
        )(ik, iqt, iwt, aqt, kh, vt))
    return jnp.concatenate(outs, axis=2)


def _sidx_body(pt_ref, qi_ref, iw_ref, iknew_ref, cache_ref, out_ref, buf_ref, sem_ref, *, n_pages):
    b = pl.program_id(0)
    nb = pl.num_programs(0)
    slot = b % 2
    past = n_pages * PAGE_SIZE

    def copies(seq, s):
        return [pltpu.make_async_copy(cache_ref.at[pt_ref[seq, p]],
                                      buf_ref.at[s, :, pl.ds(p * PAGE_SIZE, PAGE_SIZE)], sem_ref.at[s])
                for p in range(n_pages)]

    @pl.when(b == 0)
    def _():
        for c in copies(0, 0):
            c.start()

    @pl.when(b + 1 < nb)
    def _():
        for c in copies(b + 1, 1 - slot):
            c.start()

    for c in copies(b, slot):
        c.wait()
    lane = lax.broadcasted_iota(I32, (buf_ref.shape[1], LANES), 1)
    buf_ref[slot, :, past:] = jnp.where(lane == 0, iknew_ref[...], 0.0)

    s = _dot_hp(qi_ref[...], buf_ref[slot])
    out_ref[...] = jnp.sum(jnp.maximum(s, 0.0) * iw_ref[...], axis=0, keepdims=True)


def _sample_index_scores(page_table, qi, iw, ik_new, cache_t):
    db, n_pages = page_table.shape
    d_idx = cache_t.shape[1]
    n_keys = n_pages * PAGE_SIZE + LANES
    grid_spec = pltpu.PrefetchScalarGridSpec(
        num_scalar_prefetch=1, grid=(db,),
        in_specs=[pl.BlockSpec((None, H_IDX, d_idx), lambda i, pt: (i, 0, 0)),
                  pl.BlockSpec((None, H_IDX, 1), lambda i, pt: (i, 0, 0)),
                  pl.BlockSpec((None, d_idx, 1), lambda i, pt: (i, 0, 0)),
                  pl.BlockSpec(memory_space=pl.ANY)],
        out_specs=pl.BlockSpec((None, 1, n_keys), lambda i, pt: (i, 0, 0)),
        scratch_shapes=[pltpu.VMEM((2, d_idx, n_keys), F32), pltpu.SemaphoreType.DMA((2,))],
    )
    out = pl.pallas_call(
        functools.partial(_sidx_body, n_pages=n_pages), grid_spec=grid_spec,
        out_shape=jax.ShapeDtypeStruct((db, 1, n_keys), F32),
        compiler_params=_cparams(("arbitrary",)), name="sample_index_scores",
    )(page_table, qi, iw, ik_new, cache_t)
    return out.reshape(db, n_keys)


def _ssel_body(sc_ref, bias_ref, work_ref, *, n_visible, k_top):
    n, w = sc_ref.shape
    pos = lax.broadcasted_iota(I32, (n, w), 1)
    work_ref[...] = jnp.where(pos < n_visible, sc_ref[...], NEG_INF)
    sel = _select_topk(work_ref, pos, jnp.int32(n_visible), k_top, int(math.ceil(math.log2(w))), 1)
    bias_ref[...] = jnp.where(sel, 0.0, NEG_INF)


def _sample_select(scores, *, n_visible, k_top):
    return pl.pallas_call(
        functools.partial(_ssel_body, n_visible=n_visible, k_top=k_top),
        out_shape=jax.ShapeDtypeStruct(scores.shape, F32),
        scratch_shapes=[pltpu.VMEM(scores.shape, F32)],
        compiler_params=pltpu.CompilerParams(vmem_limit_bytes=VMEM_LIMIT), name="sample_select",
    )(scores)


def _sattn_body(pt_ref, q_ref, bias_ref, biasnew_ref, knew_ref, vnew_ref, ck_ref, cv_ref, o_ref,
                kbuf_ref, vbuf_ref, ksem_ref, vsem_ref, m_ref, l_ref, acc_ref, *, group):
    b = pl.program_id(0)
    g = pl.program_id(1)
    ng = pl.num_programs(1)
    step = b * ng + g
    total = pl.num_programs(0) * ng
    slot = step % 2
    n_heads, d_head, _ = q_ref.shape

    def copies(seq, grp, s):
        out = []
        for p in range(group):
            page = pt_ref[seq, grp * group + p]
            out.append(pltpu.make_async_copy(ck_ref.at[page], kbuf_ref.at[s, p], ksem_ref.at[s]))
            out.append(pltpu.make_async_copy(cv_ref.at[page], vbuf_ref.at[s, p], vsem_ref.at[s]))
        return out

    @pl.when(step == 0)
    def _():
        for c in copies(0, 0, 0):
            c.start()

    @pl.when(step + 1 < total)
    def _():
        nxt = step + 1
        for c in copies(nxt // ng, nxt % ng, 1 - slot):
            c.start()

    for c in copies(b, g, slot):
        c.wait()

    @pl.when(g == 0)
    def _():
        m_ref[...] = jnp.full(m_ref.shape, NEG_INF, F32)
        l_ref[...] = jnp.zeros(l_ref.shape, F32)
        acc_ref[...] = jnp.zeros(acc_ref.shape, F32)

    def online_update(h, scores, values):
        top = scores[0]
        for s in scores[1:]:
            top = jnp.maximum(top, s)
        m_old = m_ref[h]
        m_new = jnp.maximum(m_old, jnp.max(top, axis=-1, keepdims=True))
        m_safe = jnp.where(m_new == NEG_INF, 0.0, m_new)
        alpha = jnp.exp(m_old - m_safe)
        acc = alpha * acc_ref[h]
        l = alpha * l_ref[h]
        for s, value in zip(scores, values):
            p = jnp.exp(s - m_safe)
            l = l + jnp.sum(p, axis=-1, keepdims=True)
            acc = acc + value(p)
        m_ref[h] = m_new
        l_ref[h] = l
        acc_ref[h] = acc

    for h in range(n_heads):
        q = q_ref[h]
        scores = [jnp.sum(kbuf_ref[slot, p, h] * q, axis=0, keepdims=True) + bias_ref[p:p + 1, :]
                  for p in range(group)]
        values = [lambda pr, p=p, h=h: pr * vbuf_ref[slot, p, h] for p in range(group)]
        online_update(h, scores, values)

    @pl.when(g == ng - 1)
    def _():
        first_lane = lax.broadcasted_iota(I32, (d_head, LANES), 1) == 0
        for h in range(n_heads):
            s_new = jnp.sum(q_ref[h] * knew_ref[h], axis=0, keepdims=True) + biasnew_ref[:, 0:1]
            online_update(h, [s_new], [lambda pr, h=h: jnp.where(first_lane, pr * vnew_ref[h], 0.0)])
            o_ref[h] = jnp.sum(acc_ref[h], axis=-1, keepdims=True) / l_ref[h]


def _sample_attention(page_table, q, bias, k_new, v_new, cache_kt, cache_vt, *, group):
    db, n_pages = page_table.shape
    _, n_heads, d_head, _ = q.shape
    ng = n_pages // group
    past = n_pages * PAGE_SIZE
    col = pl.BlockSpec((None, n_heads, d_head, 1), lambda i, j, pt: (i, 0, 0, 0))
    any_spec = pl.BlockSpec(memory_space=pl.ANY)
    grid_spec = pltpu.PrefetchScalarGridSpec(
        num_scalar_prefetch=1, grid=(db, ng),
        in_specs=[col,
                  pl.BlockSpec((None, None, group, PAGE_SIZE), lambda i, j, pt: (i, j, 0, 0)),
                  pl.BlockSpec((None, 1, LANES), lambda i, j, pt: (i, 0, 0)),
                  col, col, any_spec, any_spec],
        out_specs=col,
        scratch_shapes=[pltpu.VMEM((2, group, n_heads, d_head, PAGE_SIZE), F32),
                        pltpu.VMEM((2, group, n_heads, d_head, PAGE_SIZE), F32),
                        pltpu.SemaphoreType.DMA((2,)), pltpu.SemaphoreType.DMA((2,)),
                        pltpu.VMEM((n_heads, 1, 1), F32), pltpu.VMEM((n_heads, 1, 1), F32),
                        pltpu.VMEM((n_heads, d_head, LANES), F32)],
    )
    return pl.pallas_call(
        functools.partial(_sattn_body, group=group), grid_spec=grid_spec,
        out_shape=jax.ShapeDtypeStruct((db, n_heads, d_head, 1), F32),
        compiler_params=_cparams(("arbitrary", "arbitrary")), name="sample_attention",
    )(page_table, q, bias[:, :past].reshape(db, ng, group, PAGE_SIZE), bias[:, past:].reshape(db, 1, LANES),
      k_new, v_new, cache_kt, cache_vt)


def _outproj_body(x_ref, ret_ref, att_ref, wo_ref, g1_ref, sc2_ref, sh2_ref, n2_ref, rw_ref,
                  rb_ref, cnt_in_ref, x1_ref, h2_ref, route_ref, cnt_out_ref, cnt_ref, *, n_experts, precise):
    half = ret_ref.shape[1]
    mm = _dot_hp if precise else _dot
    y = mm(ret_ref[...], wo_ref[:half, :]) + mm(att_ref[...], wo_ref[half:, :])
    x1 = x_ref[...] + g1_ref[...] * y
    x1_ref[...] = x1
    ms = jnp.mean(x1 * x1, axis=-1, keepdims=True)
    h2 = x1 * lax.rsqrt(ms + EPS) * n2_ref[...] * (1.0 + sc2_ref[...]) + sh2_ref[...]
    _store_row_tiles(h2_ref, h2)
    hi, lo = _split_bf16(h2)
    rw_hi, rw_lo = _split_bf16(rw_ref[...])
    logits = _dot(hi, rw_hi) + _dot(lo, rw_hi) + _dot(hi, rw_lo) + rb_ref[...]

    tm = logits.shape[0]
    lane = lax.broadcasted_iota(I32, (tm, LANES), 1).astype(F32)
    work = jnp.where(lane < n_experts, logits, NEG_INF)
    top_vals, top_ids, top_hot = [], [], []
    for _ in range(TOP_K):
        m = jnp.max(work, axis=-1, keepdims=True)
        first = jnp.min(jnp.where(work == m, lane, float(LANES)), axis=-1, keepdims=True)
        hot = lane == first
        top_vals.append(m)
        top_ids.append(first)
        top_hot.append(hot)
        work = jnp.where(hot, NEG_INF, work)
    exps = [jnp.exp(v - top_vals[0]) for v in top_vals]
    den = exps[0]
    for e in exps[1:]:
        den = den + e

    @pl.when(pl.program_id(0) == 0)
    def _():
        cnt_ref[...] = cnt_in_ref[...]

    picks = jnp.zeros((tm, LANES), F32)
    for hot in top_hot:
        picks = picks + jnp.where(hot, 1.0, 0.0)
    earlier = lax.broadcasted_iota(I32, (tm, tm), 1) < lax.broadcasted_iota(I32, (tm, tm), 0)
    before = _dot(jnp.where(earlier, 1.0, 0.0).astype(BF16), picks.astype(BF16)) + cnt_ref[...]
    cnt_ref[...] += jnp.sum(picks, axis=0, keepdims=True)
    cnt_out_ref[...] = cnt_ref[...]

    route = jnp.zeros((tm, LANES), F32)
    for k in range(TOP_K):
        rank = jnp.sum(jnp.where(top_hot[k], before, 0.0), axis=-1, keepdims=True)
        route = jnp.where(lane == k, top_ids[k], route)
        route = jnp.where(lane == TOP_K + k, exps[k] / den, route)
        route = jnp.where(lane == 2 * TOP_K + k, rank, route)
    route_ref[...] = route


def _outproj(x2d, ret, att, wo, g1, sc2, sh2, n2, rw, rb, cnt_in, *, tm, rows_per_batch, per_row,
             n_experts, precise):
    n, d = x2d.shape
    half = ret.shape[1]
    row = lambda i: (i, 0)
    const = lambda i: (0, 0)
    if per_row:
        mod_spec = pl.BlockSpec((tm, d), row)
    else:
        tiles_per_batch = rows_per_batch // tm
        mod_spec = pl.BlockSpec((None, 1, d), lambda i: (i // tiles_per_batch, 0, 0))
    return pl.pallas_call(
        functools.partial(_outproj_body, n_experts=n_experts, precise=precise), grid=(n // tm,),
        in_specs=[pl.BlockSpec((tm, d), row), pl.BlockSpec((tm, half), row), pl.BlockSpec((tm, half), row),
                  pl.BlockSpec(wo.shape, const), mod_spec, mod_spec, mod_spec, pl.BlockSpec((1, d), const),
                  pl.BlockSpec(rw.shape, const), pl.BlockSpec((1, LANES), const),
                  pl.BlockSpec((1, LANES), const)],
        out_specs=[pl.BlockSpec((tm, d), row), pl.BlockSpec((tm, d // LANES, LANES), lambda i: (i, 0, 0)),
                   pl.BlockSpec((tm, LANES), row),
                   pl.BlockSpec((1, LANES), const)],
        out_shape=[jax.ShapeDtypeStruct((n, d), F32), jax.ShapeDtypeStruct((n, d // LANES, LANES), F32),
                   jax.ShapeDtypeStruct((n, LANES), F32), jax.ShapeDtypeStruct((1, LANES), F32)],
        scratch_shapes=[pltpu.VMEM((1, LANES), F32)],
        compiler_params=_cparams(("arbitrary",)), name="outproj",
    )(x2d, ret, att, wo, g1, sc2, sh2, n2, rw, rb, cnt_in)


EXPERT_TILE = 256


def _store_row_tiles(ref, x):
    for c in range(ref.shape[1]):
        ref[:, c, :] = x[:, c * LANES:(c + 1) * LANES]


def _load_row_tiles(ref):
    return jnp.concatenate([ref[:, c, :] for c in range(ref.shape[1])], axis=1)


def _experts_body(te_ref, src_ref, nused_ref, h_ref, wgu_ref, bgu_ref, wd_ref, bd_ref, y_ref,
                  xbuf_ref, sem_ref, wgu_bf_ref, wd_bf_ref, *, f_chunk):
    t = pl.program_id(0)
    n_tiles = pl.num_programs(0)
    slot = t % 2
    n_used = nused_ref[0]
    tm = y_ref.shape[0]
    d_ff = wd_ref.shape[0]
    n_chunks = d_ff // f_chunk

    def row_copy(token, s, r):
        return pltpu.make_async_copy(h_ref.at[token], xbuf_ref.at[s, r], sem_ref.at[s])

    def start_rows(tile, s, rows):
        for r in rows:
            row_copy(src_ref[tile * tm + r], s, r).start(priority=r % 2)

    def wait_rows(s):
        for r in range(tm):
            row_copy(0, s, r).wait()

    @pl.when((t == 0) & (n_used > 0))
    def _():
        start_rows(0, 0, range(tm))

    @pl.when((t == n_used) & (t > 0))
    def _():
        wait_rows(slot)

    @pl.when(t >= n_used)
    def _():
        y_ref[...] = jnp.zeros(y_ref.shape, F32)

    @pl.when(t < n_used)
    def _():
        wait_rows(slot)

        @pl.when((t == 0) | (te_ref[t] != te_ref[jnp.maximum(t - 1, 0)]))
        def _():
            wgu_bf_ref[...] = wgu_ref[...].astype(BF16)
            wd_bf_ref[...] = wd_ref[...].astype(BF16)

        nxt = jnp.minimum(t + 1, n_tiles - 1)
        x = _load_row_tiles(xbuf_ref.at[slot]).astype(BF16)
        y = jnp.zeros((tm, wd_ref.shape[1]), F32)
        for j in range(n_chunks):
            cols = slice(j * f_chunk, (j + 1) * f_chunk)
            ucols = slice(d_ff + j * f_chunk, d_ff + (j + 1) * f_chunk)
            hg = _dot(x, wgu_bf_ref[:, cols]) + bgu_ref[:, cols]
            hu = _dot(x, wgu_bf_ref[:, ucols]) + bgu_ref[:, ucols]
            gate = jnp.minimum(hg, SWIGLU_LIMIT)
            up = jnp.clip(hu, -SWIGLU_LIMIT, SWIGLU_LIMIT)
            act = (up + 1.0) * gate * _sigmoid(SWIGLU_ALPHA * gate)
            y = y + _dot(act.astype(BF16), wd_bf_ref[cols, :])
            start_rows(nxt, 1 - slot, range(j * tm // n_chunks, (j + 1) * tm // n_chunks))
        _store_row_tiles(y_ref, y + bd_ref[...])

        @pl.when(t == n_tiles - 1)
        def _():
            wait_rows(1 - slot)


def _experts(tile_expert, src_token, n_used, h2, wgu, bgu, wd, bd, *, f_chunk):
    n_tiles = tile_expert.shape[0]
    _, d_tiles, _ = h2.shape
    d = d_tiles * LANES
    _, _, two_f = wgu.shape
    d_ff = two_f // 2
    by_expert = lambda r, c: pl.BlockSpec((None, r, c), lambda t, te, src, nu: (te[t], 0, 0))
    grid_spec = pltpu.PrefetchScalarGridSpec(
        num_scalar_prefetch=3, grid=(n_tiles,),
        in_specs=[pl.BlockSpec(memory_space=pl.ANY), by_expert(d, two_f), by_expert(1, two_f),
                  by_expert(d_ff, d), by_expert(1, d)],
        out_specs=pl.BlockSpec((EXPERT_TILE, d_tiles, LANES), lambda t, te, src, nu: (t, 0, 0)),
        scratch_shapes=[pltpu.VMEM((2, EXPERT_TILE, d_tiles, LANES), F32), pltpu.SemaphoreType.DMA((2,)),
                        pltpu.VMEM((d, two_f), BF16), pltpu.VMEM((d_ff, d), BF16)],
    )
    return pl.pallas_call(
        functools.partial(_experts_body, f_chunk=f_chunk), grid_spec=grid_spec,
        out_shape=jax.ShapeDtypeStruct((n_tiles * EXPERT_TILE, d_tiles, LANES), F32),
        compiler_params=_cparams(("arbitrary",)), name="experts",
    )(tile_expert, src_token, n_used, h2, wgu, bgu, wd, bd)


def _combine_body(pos_ref, y_ref, route_ref, x1_ref, g2_ref, out_ref, ybuf_ref, sem_ref, *, first_token, n_tokens):
    i = pl.program_id(0)
    slot = i % 2
    tc = out_ref.shape[0]

    def row_copy(pos, s, r, k):
        return pltpu.make_async_copy(y_ref.at[pos], ybuf_ref.at[s, k, r], sem_ref.at[s])

    def gather(tile, s):
        for k in range(TOP_K):
            for r in range(tc):
                row_copy(pos_ref[k * n_tokens + first_token + tile * tc + r], s, r, k).start(priority=r % 2)

    @pl.when(i == 0)
    def _():
        gather(0, 0)

    @pl.when(i + 1 < pl.num_programs(0))
    def _():
        gather(i + 1, 1 - slot)

    for k in range(TOP_K):
        for r in range(tc):
            row_copy(0, slot, r, k).wait()

    ff = jnp.zeros(out_ref.shape, F32)
    for k in range(TOP_K):
        ff = ff + route_ref[:, TOP_K + k:TOP_K + k + 1] * _load_row_tiles(ybuf_ref.at[slot, k])
    out_ref[...] = x1_ref[...] + g2_ref[...] * ff


def _combine(pos_flat, y, route, x1, g2, *, tc, rows_per_batch, per_row, first_token, n_tokens):
    n, d = x1.shape
    row = lambda i, pos: (i, 0)
    if per_row:
        mod_spec = pl.BlockSpec((tc, d), row)
    else:
        tiles_per_batch = rows_per_batch // tc
        mod_spec = pl.BlockSpec((None, 1, d), lambda i, pos: (i // tiles_per_batch, 0, 0))
    grid_spec = pltpu.PrefetchScalarGridSpec(
        num_scalar_prefetch=1, grid=(n // tc,),
        in_specs=[pl.BlockSpec(memory_space=pl.ANY), pl.BlockSpec((tc, LANES), row), pl.BlockSpec((tc, d), row),
                  mod_spec],
        out_specs=pl.BlockSpec((tc, d), row),
        scratch_shapes=[pltpu.VMEM((2, TOP_K, tc, d // LANES, LANES), F32), pltpu.SemaphoreType.DMA((2,))],
    )
    return pl.pallas_call(
        functools.partial(_combine_body, first_token=first_token, n_tokens=n_tokens), grid_spec=grid_spec,
        out_shape=jax.ShapeDtypeStruct((n, d), F32),
        compiler_params=_cparams(("arbitrary",)), name="combine",
    )(pos_flat, y, route, x1, g2)


def _dispatch_plan(route, counts, n_experts, n_tiles):
    n = route.shape[0]
    ids = route[:, :TOP_K].astype(I32)
    ranks = route[:, 2 * TOP_K:3 * TOP_K].astype(I32)
    cnt = counts[0, :n_experts].astype(I32)
    tiles_per_expert = (cnt + EXPERT_TILE - 1) // EXPERT_TILE
    tile_end = jnp.cumsum(tiles_per_expert)
    row_start = (tile_end - tiles_per_expert) * EXPERT_TILE
    pos = (row_start[ids] + ranks).T
    n_used = tile_end[-1]
    tile = jnp.arange(n_tiles, dtype=I32)
    last = jnp.minimum(tile, n_used - 1)
    tile_expert = jnp.sum((tile_end[None, :] <= last[:, None]).astype(I32), axis=1)
    token = jnp.broadcast_to(jnp.arange(n, dtype=I32)[None, :], pos.shape)
    src_token = jnp.zeros((n_tiles * EXPERT_TILE,), I32).at[pos.reshape(-1)].set(token.reshape(-1))
    return pos.reshape(-1), tile_expert, src_token, n_used.reshape(1).astype(I32)


def _rope_tables(pos, d):
    pos = np.asarray(pos, np.float64)
    inv = np.power(ROPE_THETA, -np.arange(0, d, 2, dtype=np.float64) / d)
    ang = pos[:, None] * inv[None, :]
    cos = np.cos(np.concatenate([ang, ang], axis=-1))
    sin = np.sin(ang)
    sin = np.concatenate([-sin, sin], axis=-1)
    reps = LANES // d
    return jnp.asarray(np.tile(cos, (1, reps)), F32), jnp.asarray(np.tile(sin, (1, reps)), F32)


def _split_bf16(x):
    hi = x.astype(BF16)
    return hi, (x - hi.astype(F32)).astype(BF16)


def _adaln_body(c_ref, w_ref, b_ref, o_ref):
    c = c_ref[...]
    a_hi, a_lo = _split_bf16(c * _sigmoid(c))
    w_hi, w_lo = _split_bf16(w_ref[...])
    o_ref[...] = _dot(a_hi, w_hi) + _dot(a_lo, w_hi) + _dot(a_hi, w_lo) + b_ref[...]


def _adaln(c, ada_w, ada_b):
    n, d = c.shape
    n_out = ada_w.shape[1]
    tn = d
    out = pl.pallas_call(
        _adaln_body, grid=(n_out // tn,),
        in_specs=[pl.BlockSpec((n, d), lambda j: (0, 0)), pl.BlockSpec((d, tn), lambda j: (0, j)),
                  pl.BlockSpec((1, tn), lambda j: (0, j))],
        out_specs=pl.BlockSpec((n, tn), lambda j: (0, j)),
        out_shape=jax.ShapeDtypeStruct((n, n_out), F32),
        compiler_params=_cparams(("arbitrary",)), name="adaln",
    )(c, ada_w, ada_b[None])
    return jnp.split(out, 6, axis=-1)


def kernel(x_prompt, x_sample, cache_k, cache_v, cache_idx_k, state_ret, page_table, c_prompt, c_sample,
           ada_w, ada_b, norm1_w, w_in, q_norm_w, k_norm_w, idx_k_norm_w, ret_gn_w, w_out, norm2_w,
           router_w, router_b, w_gate_up, b_gate_up, w_down, b_down):
    depth = w_in.shape[0]
    b, t, d = x_prompt.shape
    db, ts, _ = x_sample.shape
    assert ts == 1, "the sample group decodes one token per sequence"
    _, _, h_ret, d_ret, _ = state_ret.shape
    _, n_pool, page, h_att, d_head = cache_k.shape
    assert page == PAGE_SIZE
    d_idx = cache_idx_k.shape[-1]
    ret_w, att_w = h_ret * d_ret, h_att * d_head
    assert ret_w == att_w == H_IDX * d_idx and d_ret == d_head == d_idx == 64
    n_experts = router_w.shape[-1]
    n_pages = page_table.shape[1]
    past = n_pages * PAGE_SIZE
    n_main = 8 * ret_w

    cos_p, sin_p = _rope_tables(np.arange(t), d_ret)
    cos_s, sin_s = _rope_tables(np.full((db,), past), d_ret)
    log_g = jnp.log1p(-jnp.power(2.0, -5.0 - jnp.arange(h_ret, dtype=F32)))
    gmat = jnp.asarray(np.kron(np.eye(ret_w // d_head), np.full((d_head, d_head), 1.0 / d_head)), BF16)

    hp, hs = x_prompt.reshape(b * t, d), x_sample.reshape(db, d)
    outs = [[] for _ in range(8)]
    for l in range(depth):
        w_main = w_in[l, :, :n_main]
        w_tail = jnp.pad(w_in[l, :, n_main:], ((0, 0), (0, LANES - (w_in.shape[2] - n_main))))
        qn = jnp.tile(q_norm_w[l], h_att)[None]
        kn = jnp.tile(k_norm_w[l], h_att)[None]
        ikn = jnp.pad(idx_k_norm_w[l], (0, LANES - d_idx))[None]
        gnw = ret_gn_w[l].reshape(h_ret, 1, d_ret)
        wo = w_out[l]
        rw = jnp.pad(router_w[l], ((0, 0), (0, LANES - n_experts)))
        rb = jnp.pad(router_b[l], (0, LANES - n_experts))[None]
        n1, n2 = norm1_w[l][None], norm2_w[l][None]
        bgu, bd = b_gate_up[l][:, None, :], b_down[l][:, None, :]
        inproj = functools.partial(_inproj, n1=n1, qn=qn, kn=kn, ikn=ikn, gmat=gmat,
                                   d_ret=d_ret, d_head=d_head, d_idx=d_idx)
        outproj = functools.partial(_outproj, n2=n2, rw=rw, rb=rb, n_experts=n_experts)

        mods = _adaln(jnp.concatenate([c_prompt, c_sample], axis=0), ada_w[l], ada_b[l])

        sh1, sc1, g1, sh2, sc2, g2 = [a[:b, None, :] for a in mods]
        rq, rk, rv, rg, aq, ak, akb, av, avb, iq, tail = inproj(
            hp, sc1, sh1, w_main=w_main.astype(BF16), w_tail=w_tail.astype(BF16), cos=cos_p, sin=sin_p,
            tm=256, rows_per_batch=t, per_row=False, precise=False)

        heads = lambda a, nh: a.reshape(b, t, nh, -1).transpose(0, 2, 1, 3)
        ret_o, ret_s = _ret_prompt(log_g, heads(rq, h_ret), heads(rk, h_ret).swapaxes(2, 3), heads(rv, h_ret),
                                   heads(rg, h_ret), gnw, chunk=256)
        ret_o = ret_o.transpose(0, 2, 1, 3).reshape(b * t, ret_w)

        tail3 = tail.reshape(b, t, LANES)
        ikp = tail3[:, :, :d_idx]
        seq_t = lambda a: a.reshape(b, t, -1).swapaxes(1, 2)
        iwt = tail3[:, :, d_idx:d_idx + H_IDX].swapaxes(1, 2) * (d_idx ** -0.5)
        att_t = _dsa_prompt(ikp.astype(BF16), seq_t(iq), iwt, seq_t(aq), heads(akb, h_att), seq_t(avb),
                            qb=128, k_top=min(TOPK_MAX, t // 4), n_splits=4)
        att_o = att_t.swapaxes(1, 2).reshape(b * t, att_w)

        x1_p, h2_p, route_p, counts = outproj(hp, ret_o, att_o, wo=wo.astype(BF16), g1=g1, sc2=sc2, sh2=sh2,
                                              cnt_in=jnp.zeros((1, LANES), F32), tm=512, rows_per_batch=t,
                                              per_row=False, precise=False)
        g2_p = g2
        outs[0].append(ak.reshape(b, t, h_att, d_head))
        outs[1].append(av.reshape(b, t, h_att, d_head))
        outs[2].append(ikp)
        outs[3].append(ret_s)

        sh1, sc1, g1, sh2, sc2, g2 = [a[b:] for a in mods]
        rq, rk, rv, rg, aq, ak, akb, av, avb, iq, tail = inproj(
            hs, sc1, sh1, w_main=w_main, w_tail=w_tail, cos=cos_s, sin=sin_s, tm=db, rows_per_batch=1,
            per_row=True, precise=True)
        hd = lambda a: a.reshape(db, h_ret, d_ret)
        ret_o, ret_s = _ret_sample(log_g, hd(rq), hd(rk), hd(rv), hd(rg), state_ret[l], gnw)

        iks = tail[:, :d_idx]
        iw = tail[:, d_idx:d_idx + H_IDX] * (d_idx ** -0.5)
        scores = _sample_index_scores(page_table, iq.reshape(db, H_IDX, d_idx), iw[:, :, None], iks[:, :, None],
                                      cache_idx_k[l].transpose(0, 2, 1))
        bias = _sample_select(scores, n_visible=past + ts, k_top=min(TOPK_MAX, (past + ts) // 4))
        hda = lambda a: a.reshape(db, h_att, d_head, 1)
        att_o = _sample_attention(page_table, hda(aq), bias, hda(ak), hda(av), cache_k[l].transpose(0, 2, 3, 1),
                                  cache_v[l].transpose(0, 2, 3, 1), group=8)

        x1_s, h2_s, route_s, counts = outproj(hs, ret_o, att_o.reshape(db, att_w), wo=wo, g1=g1, sc2=sc2, sh2=sh2,
                                              cnt_in=counts, tm=db, rows_per_batch=1, per_row=True, precise=True)

        n_all = b * t + db
        n_tiles = -(-TOP_K * n_all // EXPERT_TILE) + n_experts
        pos, tile_expert, src_token, n_used = _dispatch_plan(
            jnp.concatenate([route_p, route_s], axis=0), counts, n_experts, n_tiles)
        y = _experts(tile_expert, src_token, n_used, jnp.concatenate([h2_p, h2_s], axis=0),
                     w_gate_up[l], bgu, w_down[l], bd, f_chunk=512)
        hp = _combine(pos, y, route_p, x1_p, g2_p, tc=128, rows_per_batch=t, per_row=False, first_token=0,
                      n_tokens=n_all)
        hs = _combine(pos, y, route_s, x1_s, g2, tc=db, rows_per_batch=1, per_row=True, first_token=b * t,
                      n_tokens=n_all)
        outs[4].append(ak.reshape(db, ts, h_att, d_head))
        outs[5].append(av.reshape(db, ts, h_att, d_head))
        outs[6].append(iks.reshape(db, ts, d_idx))
        outs[7].append(ret_s)

    k_p, v_p, ik_p, r_p, k_s, v_s, ik_s, r_s = [jnp.stack(o) for o in outs]
    return (hp.reshape(b, t, d), hs.reshape(db, ts, d), k_p, v_p, ik_p, r_p, k_s, v_s, ik_s, r_s)
```

```python
import functools
import math

import numpy as np
import jax
import jax.numpy as jnp
from jax import lax
from jax.experimental import pallas as pl
from jax.experimental.pallas import tpu as pltpu

F32 = jnp.float32
BF16 = jnp.bfloat16
I32 = jnp.int32

PAGE_SIZE = 128
H_IDX = 8
TOPK_MAX = 256
TOP_K = 4
SWIGLU_LIMIT = 7.0
SWIGLU_ALPHA = 1.702
ROPE_THETA = 10000.0
EPS = 1e-6
INT_MIN = -(2 ** 31)
NEG_INF = float("-inf")

LANES = 128
VMEM_LIMIT = 56 * 1024 * 1024


def _cparams(sem):
    return pltpu.CompilerParams(dimension_semantics=sem, vmem_limit_bytes=VMEM_LIMIT)


def _dot(a, b):
    return jnp.dot(a, b, preferred_element_type=F32)


def _dot_nt(a, b):
    return lax.dot_general(a, b, (((1,), (1,)), ((), ())), preferred_element_type=F32)


def _sigmoid(x):
    return 1.0 / (1.0 + jnp.exp(-x))


def _dot_hp(a, b):
    return jnp.dot(a, b, preferred_element_type=F32, precision=lax.Precision.HIGHEST)


def _dot_nt_hp(a, b):
    return lax.dot_general(a, b, (((1,), (1,)), ((), ())), preferred_element_type=F32,
                           precision=lax.Precision.HIGHEST)


def _ordered_float(key):
    bits = jnp.where(key < 0, key ^ jnp.int32(0x7FFFFFFF), key)
    return pltpu.bitcast(bits, F32)


def _rope_halves(v, cos, sin_signed, first_half, width):
    swapped = jnp.where(first_half, pltpu.roll(v, width - 32, 1), pltpu.roll(v, 32, 1))
    return v * cos + swapped * sin_signed


def _inproj_body(x_ref, sc_ref, sh_ref, n1_ref, w_ref, wt_ref, cos_ref, sin_ref, qn_ref, kn_ref,
                 ikn_ref, gmat_ref,
                 rq_ref, rk_ref, rv_ref, rg_ref, aq_ref, ak_ref, akb_ref, av_ref, avb_ref, iq_ref,
                 tail_ref, *, seg_w, d_idx, iw_scale, k_scale, q_scale, precise):
    x = x_ref[...]
    tm = x.shape[0]
    ms = jnp.mean(x * x, axis=-1, keepdims=True)
    xn = x * lax.rsqrt(ms + EPS) * n1_ref[...]
    h = xn * (1.0 + sc_ref[...]) + sh_ref[...]
    if precise:
        mm = _dot_hp
    else:
        mm = _dot
        h = h.astype(BF16)

    cos1 = cos_ref[...]
    sin1 = sin_ref[...]
    reps = seg_w // LANES
    cos = jnp.concatenate([cos1] * reps, axis=1)
    sin = jnp.concatenate([sin1] * reps, axis=1)
    lane = lax.broadcasted_iota(I32, (tm, seg_w), 1)
    first = (lane & 32) == 0
    rope = functools.partial(_rope_halves, cos=cos, sin_signed=sin, first_half=first, width=seg_w)

    def seg(j):
        return mm(h, w_ref[:, j * seg_w:(j + 1) * seg_w])

    def head_norm(v, w):
        sq = v * v
        hi = sq.astype(BF16)
        lo = (sq - hi.astype(F32)).astype(BF16)
        msq = _dot(hi, gmat_ref[...]) + _dot(lo, gmat_ref[...])
        return v * lax.rsqrt(msq + EPS) * w

    rq_ref[...] = rope(seg(0)).astype(rq_ref.dtype)
    rk_ref[...] = (rope(seg(1)) * k_scale).astype(rk_ref.dtype)
    rv_ref[...] = seg(2).astype(rv_ref.dtype)
    rg_ref[...] = seg(3).astype(rg_ref.dtype)
    aq_ref[...] = (rope(head_norm(seg(4), qn_ref[...])) * q_scale).astype(aq_ref.dtype)
    ak = rope(head_norm(seg(5), kn_ref[...]))
    ak_ref[...] = ak
    akb_ref[...] = ak.astype(BF16)
    av = seg(6)
    av_ref[...] = av
    avb_ref[...] = av.astype(BF16)
    iq_ref[...] = rope(seg(7)).astype(iq_ref.dtype)

    zt = mm(h, wt_ref[...])
    lane_t = lax.broadcasted_iota(I32, (tm, LANES), 1)
    is_key = lane_t < d_idx
    msk = jnp.sum(jnp.where(is_key, zt * zt, 0.0), axis=-1, keepdims=True) * (1.0 / d_idx)
    kn = zt * lax.rsqrt(msk + EPS) * ikn_ref[...]
    kr = _rope_halves(kn, cos1, sin1, (lane_t & 32) == 0, LANES)
    tail_ref[...] = jnp.where(is_key, kr, zt * iw_scale)


def _inproj(x2d, sc, sh, n1, w_main, w_tail, cos, sin, qn, kn, ikn, gmat, *, tm, rows_per_batch,
            per_row, precise, d_ret, d_head, d_idx):
    act_dtype = F32 if precise else BF16
    n, d = x2d.shape
    seg_w = w_main.shape[1] // 8
    grid = (n // tm,)
    row = lambda i: (i, 0)
    const = lambda i: (0, 0)
    if per_row:
        mod_spec = pl.BlockSpec((tm, d), row)
        pos_spec = pl.BlockSpec((tm, LANES), row)
    else:
        tiles_per_batch = rows_per_batch // tm
        mod_spec = pl.BlockSpec((None, 1, d), lambda i: (i // tiles_per_batch, 0, 0))
        pos_spec = pl.BlockSpec((tm, LANES), lambda i: (i % tiles_per_batch, 0))
    in_specs = [
        pl.BlockSpec((tm, d), row), mod_spec, mod_spec, pl.BlockSpec((1, d), const),
        pl.BlockSpec(w_main.shape, const), pl.BlockSpec(w_tail.shape, const),
        pos_spec, pos_spec,
        pl.BlockSpec((1, seg_w), const), pl.BlockSpec((1, seg_w), const), pl.BlockSpec((1, LANES), const),
        pl.BlockSpec(gmat.shape, const),
    ]
    seg_spec = pl.BlockSpec((tm, seg_w), row)
    out_dtypes = [act_dtype, act_dtype, act_dtype, act_dtype, act_dtype, F32, BF16, F32, BF16, act_dtype]
    out_shape = [jax.ShapeDtypeStruct((n, seg_w), dt) for dt in out_dtypes]
    out_shape.append(jax.ShapeDtypeStruct((n, LANES), F32))
    out_specs = [seg_spec] * 10 + [pl.BlockSpec((tm, LANES), row)]
    body = functools.partial(_inproj_body, seg_w=seg_w, d_idx=d_idx, iw_scale=H_IDX ** -0.5,
                             k_scale=d_ret ** -0.5, q_scale=d_head ** -0.5, precise=precise)
    return pl.pallas_call(
        body, grid=grid, in_specs=in_specs, out_specs=out_specs, out_shape=out_shape,
        compiler_params=_cparams(("arbitrary",)), name="inproj",
    )(x2d, sc, sh, n1, w_main, w_tail, cos, sin, qn, kn, ikn, gmat)


def _group_norm_gate(o, g, gnw):
    mu = jnp.mean(o, axis=-1, keepdims=True)
    var = jnp.mean((o - mu) ** 2, axis=-1, keepdims=True)
    on = (o - mu) * lax.rsqrt(var + EPS) * gnw
    g = g.astype(F32)
    return g * _sigmoid(g) * on


def _ret_prompt_body(lg_ref, q_ref, kt_ref, v_ref, g_ref, gnw_ref, o_ref, s_ref, *, chunk):
    t, dk = q_ref.shape
    lg = lg_ref[pl.program_id(1)]
    ii = lax.broadcasted_iota(I32, (chunk, chunk), 0)
    jj = lax.broadcasted_iota(I32, (chunk, chunk), 1)
    rel = (ii - jj).astype(F32)
    decay = jnp.where(rel >= 0.0, jnp.exp(jnp.maximum(rel, 0.0) * lg), 0.0)
    q_dec = jnp.exp((lax.broadcasted_iota(I32, (chunk, 1), 0).astype(F32) + 1.0) * lg)
    k_dec = jnp.exp((chunk - 1.0 - lax.broadcasted_iota(I32, (1, chunk), 1).astype(F32)) * lg)
    g_chunk = jnp.exp(jnp.full((1, 1), chunk, F32) * lg)
    gnw = gnw_ref[...]
    state = jnp.zeros((dk, v_ref.shape[1]), F32)
    for c in range(t // chunk):
        rows = slice(c * chunk, (c + 1) * chunk)
        qc = q_ref[rows, :]
        ktc = kt_ref[:, rows]
        vc = v_ref[rows, :]
        inner = _dot(qc, ktc) * decay
        o = _dot(inner.astype(BF16), vc) + _dot(qc, state.astype(BF16)) * q_dec
        state = state * g_chunk + _dot((ktc.astype(F32) * k_dec).astype(BF16), vc)
        o_ref[rows, :] = _group_norm_gate(o, g_ref[rows, :], gnw).astype(o_ref.dtype)
    s_ref[...] = state


def _ret_prompt(log_g, q, kt, v, g, gnw, *, chunk):
    b, h, t, dk = q.shape
    dv = v.shape[-1]
    blk = lambda r, c: pl.BlockSpec((None, None, r, c), lambda i, j, lg: (i, j, 0, 0))
    grid_spec = pltpu.PrefetchScalarGridSpec(
        num_scalar_prefetch=1, grid=(b, h),
        in_specs=[blk(t, dk), blk(dk, t), blk(t, dv), blk(t, dv),
                  pl.BlockSpec((None, 1, dv), lambda i, j, lg: (j, 0, 0))],
        out_specs=[blk(t, dv), blk(dk, dv)],
    )
    return pl.pallas_call(
        functools.partial(_ret_prompt_body, chunk=chunk), grid_spec=grid_spec,
        out_shape=[jax.ShapeDtypeStruct((b, h, t, dv), BF16), jax.ShapeDtypeStruct((b, h, dk, dv), F32)],
        compiler_params=_cparams(("arbitrary", "arbitrary")), name="ret_prompt",
    )(log_g, q, kt, v, g, gnw)


def _ret_sample_body(lg_ref, qc_ref, kc_ref, v_ref, g_ref, s0_ref, gnw_ref, o_ref, s_ref):
    for h in range(qc_ref.shape[0]):
        gamma = jnp.exp(jnp.full((1, 1), 1.0, F32) * lg_ref[h])
        qc = qc_ref[h]
        kc = kc_ref[h]
        v = v_ref[h]
        s0 = s0_ref[h]
        qk = jnp.sum(qc * kc, axis=0, keepdims=True)
        o = qk * v + jnp.sum(qc * s0, axis=0, keepdims=True) * gamma
        s_ref[h] = s0 * gamma + kc * v
        o_ref[h] = _group_norm_gate(o, g_ref[h], gnw_ref[h]).astype(o_ref.dtype)


def _ret_sample(log_g, q, k, v, g, s0, gnw):
    db, h, dk = q.shape
    dv = v.shape[-1]
    col = pl.BlockSpec((None, h, dk, 1), lambda i, lg: (i, 0, 0, 0))
    rowv = pl.BlockSpec((None, h, 1, dv), lambda i, lg: (i, 0, 0, 0))
    mat = pl.BlockSpec((None, h, dk, dv), lambda i, lg: (i, 0, 0, 0))
    grid_spec = pltpu.PrefetchScalarGridSpec(
        num_scalar_prefetch=1, grid=(db,),
        in_specs=[col, col, rowv, rowv, mat, pl.BlockSpec((h, 1, dv), lambda i, lg: (0, 0, 0))],
        out_specs=[rowv, mat],
    )
    o, s = pl.pallas_call(
        _ret_sample_body, grid_spec=grid_spec,
        out_shape=[jax.ShapeDtypeStruct((db, h, 1, dv), F32), jax.ShapeDtypeStruct((db, h, dk, dv), F32)],
        compiler_params=_cparams(("arbitrary",)), name="ret_sample",
    )(log_g, q.reshape(db, h, dk, 1), k.reshape(db, h, dk, 1), v.reshape(db, h, 1, dv),
      g.reshape(db, h, 1, dv), s0, gnw)
    return o.reshape(db, h * dv), s


ROW_CHAINS = 64
KEY_CHUNK = 256


def _reduce_rows(x, combine, finish):
    rows = x.shape[0]
    width = math.gcd(rows, ROW_CHAINS)
    acc = x[0:width]
    for c in range(1, rows // width):
        acc = combine(acc, x[c * width:(c + 1) * width])
    return finish(acc, axis=0, keepdims=True)


def _select_topk(sc_ref, pos, n_visible, k_top, n_pos_bits, axis):
    def count(mask):
        ones = jnp.where(mask, 1.0, 0.0)
        return _reduce_rows(ones, jnp.add, jnp.sum) if axis == 0 else jnp.sum(ones, axis=axis, keepdims=True)

    kf = float(k_top)
    c0 = count(sc_ref[...] >= 0.0)
    cur = jnp.where(c0 >= kf, jnp.int32(0), jnp.int32(INT_MIN))

    def value_step(i, cur):
        cand = cur + (jnp.int32(1) << (30 - i))
        return jnp.where(count(sc_ref[...] >= _ordered_float(cand)) >= kf, cand, cur)

    thr = _ordered_float(lax.fori_loop(0, 31, value_step, cur))
    need = kf - count(sc_ref[...] > thr)

    def tie_step(i, m):
        cand = m + (jnp.int32(1) << (n_pos_bits - 1 - i))
        below = count((sc_ref[...] == thr) & (pos < cand))
        return jnp.where(below < need, cand, m)

    surplus = jnp.max(count(sc_ref[...] == thr) - need)
    cut = lax.cond(surplus > 0.0,
                   lambda: lax.fori_loop(0, n_pos_bits, tie_step, jnp.zeros(thr.shape, I32)),
                   lambda: jnp.full(thr.shape, (1 << n_pos_bits) - 1, I32))
    sc = sc_ref[...]
    picked = (sc > thr) | ((sc == thr) & (pos <= cut))
    take_all = jnp.broadcast_to(n_visible, sc.shape) <= k_top
    return (take_all & (sc > NEG_INF)) | (~take_all & picked)


def _dsa_prompt_body(ik_ref, iqt_ref, iwt_ref, aqt_ref, k_ref, vt_ref, ot_ref, sc_ref, bias_ref,
                     *, k_top, d_idx, d_head, first_qblock):
    t, qb = sc_ref.shape
    j = pl.program_id(1) + first_qblock
    krow = lax.broadcasted_iota(I32, (t, qb), 0)
    qpos = j * qb + lax.broadcasted_iota(I32, (1, qb), 1)

    kc = math.gcd(t, KEY_CHUNK)
    chunks = [slice(c * kc, (c + 1) * kc) for c in range(t // kc)]

    for rows in chunks:
        ik = ik_ref[rows, :]
        sc = jnp.zeros((kc, qb), F32)
        for h in range(iwt_ref.shape[0]):
            s = _dot(ik, iqt_ref[h * d_idx:(h + 1) * d_idx, :])
            sc = sc + jnp.maximum(s, 0.0) * iwt_ref[h:h + 1, :]
        sc_ref[rows, :] = jnp.where(krow[rows, :] <= qpos, sc, NEG_INF)
    sel = _select_topk(sc_ref, krow, qpos + 1, k_top, int(math.ceil(math.log2(t))), 0)
    bias_ref[...] = jnp.where(sel, 0.0, NEG_INF)

    for h in range(k_ref.shape[0]):
        hrows = slice(h * d_head, (h + 1) * d_head)
        s = _dot(k_ref[h], aqt_ref[hrows, :]) + bias_ref[...]
        m = _reduce_rows(s, jnp.maximum, jnp.max)
        p = jnp.exp(s - m)
        l = _reduce_rows(p, jnp.add, jnp.sum)
        o = _dot(vt_ref[hrows, :], p.astype(BF16))
        ot_ref[hrows, :] = (o / l).astype(ot_ref.dtype)


def _dsa_prompt(ik, iqt, iwt, aqt, kh, vt, *, qb, k_top, n_splits):
    b, t, d_idx = ik.shape
    n_heads, d_head = kh.shape[1], kh.shape[3]
    w = vt.shape[1]
    nq = t // qb // n_splits
    outs = []
    for s in range(n_splits):
        tk = (s + 1) * t // n_splits
        per_b = lambda r, c: pl.BlockSpec((None, r, c), lambda i, j: (i, 0, 0))
        per_q = lambda r, s=s: pl.BlockSpec((None, r, qb), lambda i, j: (i, 0, j + s * nq))
        outs.append(pl.pallas_call(
            functools.partial(_dsa_prompt_body, k_top=k_top, d_idx=d_idx, d_head=d_head, first_qblock=s * nq),
            grid=(b, nq),
            in_specs=[per_b(tk, d_idx), per_q(iqt.shape[1]), per_q(iwt.shape[1]), per_q(w),
                      pl.BlockSpec((None, n_heads, tk, d_head), lambda i, j: (i, 0, 0, 0)), per_b(w, tk)],
            out_specs=pl.BlockSpec((None, w, qb), lambda i, j: (i, 0, j)),
            out_shape=jax.ShapeDtypeStruct((b, w, nq * qb), BF16),
            scratch_shapes=[pltpu.VMEM((tk, qb), F32), pltpu.VMEM((tk, qb), F32)],
            compiler_params=_cparams(("arbitrary", "arbitrary")), name=f"dsa_prompt_{s}",
        )(ik, iqt, iwt, aqt, kh, vt))
    return jnp.concatenate(outs, axis=2)


def _sidx_body(pt_ref, qi_ref, iw_ref, iknew_ref, cache_ref, out_ref, buf_ref, sem_ref, *, n_pages):
    b = pl.program_id(0)
    nb = pl.num_programs(0)
    slot = b % 2
    past = n_pages * PAGE_SIZE

    def copies(seq, s):
        return [pltpu.make_async_copy(cache_ref.at[pt_ref[seq, p]],
                                      buf_ref.at[s, :, pl.ds(p * PAGE_SIZE, PAGE_SIZE)], sem_ref.at[s])
                for p in range(n_pages)]

    @pl.when(b == 0)
    def _():
        for c in copies(0, 0):
            c.start()

    @pl.when(b + 1 < nb)
    def _():
        for c in copies(b + 1, 1 - slot):
            c.start()

    for c in copies(b, slot):
        c.wait()
    lane = lax.broadcasted_iota(I32, (buf_ref.shape[1], LANES), 1)
    buf_ref[slot, :, past:] = jnp.where(lane == 0, iknew_ref[...], 0.0)

    s = _dot_hp(qi_ref[...], buf_ref[slot])
    out_ref[...] = jnp.sum(jnp.maximum(s, 0.0) * iw_ref[...], axis=0, keepdims=True)


def _sample_index_scores(page_table, qi, iw, ik_new, cache_t):
    db, n_pages = page_table.shape
    d_idx = cache_t.shape[1]
    n_keys = n_pages * PAGE_SIZE + LANES
    grid_spec = pltpu.PrefetchScalarGridSpec(
        num_scalar_prefetch=1, grid=(db,),
        in_specs=[pl.BlockSpec((None, H_IDX, d_idx), lambda i, pt: (i, 0, 0)),
                  pl.BlockSpec((None, H_IDX, 1), lambda i, pt: (i, 0, 0)),
                  pl.BlockSpec((None, d_idx, 1), lambda i, pt: (i, 0, 0)),
                  pl.BlockSpec(memory_space=pl.ANY)],
        out_specs=pl.BlockSpec((None, 1, n_keys), lambda i, pt: (i, 0, 0)),
        scratch_shapes=[pltpu.VMEM((2, d_idx, n_keys), F32), pltpu.SemaphoreType.DMA((2,))],
    )
    out = pl.pallas_call(
        functools.partial(_sidx_body, n_pages=n_pages), grid_spec=grid_spec,
        out_shape=jax.ShapeDtypeStruct((db, 1, n_keys), F32),
        compiler_params=_cparams(("arbitrary",)), name="sample_index_scores",
    )(page_table, qi, iw, ik_new, cache_t)
    return out.reshape(db, n_keys)


def _ssel_body(sc_ref, bias_ref, work_ref, *, n_visible, k_top):
    n, w = sc_ref.shape
    pos = lax.broadcasted_iota(I32, (n, w), 1)
    work_ref[...] = jnp.where(pos < n_visible, sc_ref[...], NEG_INF)
    sel = _select_topk(work_ref, pos, jnp.int32(n_visible), k_top, int(math.ceil(math.log2(w))), 1)
    bias_ref[...] = jnp.where(sel, 0.0, NEG_INF)


def _sample_select(scores, *, n_visible, k_top):
    return pl.pallas_call(
        functools.partial(_ssel_body, n_visible=n_visible, k_top=k_top),
        out_shape=jax.ShapeDtypeStruct(scores.shape, F32),
        scratch_shapes=[pltpu.VMEM(scores.shape, F32)],
        compiler_params=pltpu.CompilerParams(vmem_limit_bytes=VMEM_LIMIT), name="sample_select",
    )(scores)


def _sattn_body(pt_ref, q_ref, bias_ref, biasnew_ref, knew_ref, vnew_ref, ck_ref, cv_ref, o_ref,
                kbuf_ref, vbuf_ref, ksem_ref, vsem_ref, m_ref, l_ref, acc_ref, *, group):
    b = pl.program_id(0)
    g = pl.program_id(1)
    ng = pl.num_programs(1)
    step = b * ng + g
    total = pl.num_programs(0) * ng
    slot = step % 2
    n_heads, d_head, _ = q_ref.shape

    def copies(seq, grp, s):
        out = []
        for p in range(group):
            page = pt_ref[seq, grp * group + p]
            out.append(pltpu.make_async_copy(ck_ref.at[page], kbuf_ref.at[s, p], ksem_ref.at[s]))
            out.append(pltpu.make_async_copy(cv_ref.at[page], vbuf_ref.at[s, p], vsem_ref.at[s]))
        return out

    @pl.when(step == 0)
    def _():
        for c in copies(0, 0, 0):
            c.start()

    @pl.when(step + 1 < total)
    def _():
        nxt = step + 1
        for c in copies(nxt // ng, nxt % ng, 1 - slot):
            c.start()

    for c in copies(b, g, slot):
        c.wait()

    @pl.when(g == 0)
    def _():
        m_ref[...] = jnp.full(m_ref.shape, NEG_INF, F32)
        l_ref[...] = jnp.zeros(l_ref.shape, F32)
        acc_ref[...] = jnp.zeros(acc_ref.shape, F32)

    def online_update(h, scores, values):
        top = scores[0]
        for s in scores[1:]:
            top = jnp.maximum(top, s)
        m_old = m_ref[h]
        m_new = jnp.maximum(m_old, top)
        m_safe = jnp.where(m_new == NEG_INF, 0.0, m_new)
        alpha = jnp.exp(m_old - m_safe)
        acc = alpha * acc_ref[h]
        l = alpha * l_ref[h]
        for s, v in zip(scores, values):
            p = jnp.exp(s - m_safe)
            l = l + p
            acc = acc + p * v
        m_ref[h] = m_new
        l_ref[h] = l
        acc_ref[h] = acc

    def finish(h):
        m = m_ref[h]
        m_all = jnp.max(m, axis=-1, keepdims=True)
        w = jnp.exp(m - m_all)
        l_all = jnp.sum(l_ref[h] * w, axis=-1, keepdims=True)
        o_ref[h] = jnp.sum(acc_ref[h] * w, axis=-1, keepdims=True) / l_all

    for h in range(n_heads):
        q = q_ref[h]
        scores = [jnp.sum(kbuf_ref[slot, p, h] * q, axis=0, keepdims=True) + bias_ref[p:p + 1, :]
                  for p in range(group)]
        online_update(h, scores, [vbuf_ref[slot, p, h] for p in range(group)])

    @pl.when(g == ng - 1)
    def _():
        first_lane = lax.broadcasted_iota(I32, (1, LANES), 1) == 0
        for h in range(n_heads):
            s_new = jnp.sum(q_ref[h] * knew_ref[h], axis=0, keepdims=True) + biasnew_ref[:, 0:1]
            online_update(h, [jnp.where(first_lane, s_new, NEG_INF)], [vnew_ref[h]])
            finish(h)


def _sample_attention(page_table, q, bias, k_new, v_new, cache_kt, cache_vt, *, group):
    db, n_pages = page_table.shape
    _, n_heads, d_head, _ = q.shape
    ng = n_pages // group
    past = n_pages * PAGE_SIZE
    col = pl.BlockSpec((None, n_heads, d_head, 1), lambda i, j, pt: (i, 0, 0, 0))
    any_spec = pl.BlockSpec(memory_space=pl.ANY)
    grid_spec = pltpu.PrefetchScalarGridSpec(
        num_scalar_prefetch=1, grid=(db, ng),
        in_specs=[col,
                  pl.BlockSpec((None, None, group, PAGE_SIZE), lambda i, j, pt: (i, j, 0, 0)),
                  pl.BlockSpec((None, 1, LANES), lambda i, j, pt: (i, 0, 0)),
                  col, col, any_spec, any_spec],
        out_specs=col,
        scratch_shapes=[pltpu.VMEM((2, group, n_heads, d_head, PAGE_SIZE), F32),
                        pltpu.VMEM((2, group, n_heads, d_head, PAGE_SIZE), F32),
                        pltpu.SemaphoreType.DMA((2,)), pltpu.SemaphoreType.DMA((2,)),
                        pltpu.VMEM((n_heads, 1, LANES), F32), pltpu.VMEM((n_heads, 1, LANES), F32),
                        pltpu.VMEM((n_heads, d_head, LANES), F32)],
    )
    return pl.pallas_call(
        functools.partial(_sattn_body, group=group), grid_spec=grid_spec,
        out_shape=jax.ShapeDtypeStruct((db, n_heads, d_head, 1), F32),
        compiler_params=_cparams(("arbitrary", "arbitrary")), name="sample_attention",
    )(page_table, q, bias[:, :past].reshape(db, ng, group, PAGE_SIZE), bias[:, past:].reshape(db, 1, LANES),
      k_new, v_new, cache_kt, cache_vt)


def _outproj_body(x_ref, ret_ref, att_ref, wo_ref, g1_ref, sc2_ref, sh2_ref, n2_ref, rw_ref,
                  rb_ref, cnt_in_ref, x1_ref, h2_ref, route_ref, cnt_out_ref, cnt_ref, *, n_experts, precise):
    half = ret_ref.shape[1]
    mm = _dot_hp if precise else _dot
    y = mm(ret_ref[...], wo_ref[:half, :]) + mm(att_ref[...], wo_ref[half:, :])
    x1 = x_ref[...] + g1_ref[...] * y
    x1_ref[...] = x1
    ms = jnp.mean(x1 * x1, axis=-1, keepdims=True)
    h2 = x1 * lax.rsqrt(ms + EPS) * n2_ref[...] * (1.0 + sc2_ref[...]) + sh2_ref[...]
    _store_row_tiles(h2_ref, h2)
    hi, lo = _split_bf16(h2)
    rw_hi, rw_lo = _split_bf16(rw_ref[...])
    logits = _dot(hi, rw_hi) + _dot(lo, rw_hi) + _dot(hi, rw_lo) + rb_ref[...]

    tm = logits.shape[0]
    lane = lax.broadcasted_iota(I32, (tm, LANES), 1).astype(F32)
    work = jnp.where(lane < n_experts, logits, NEG_INF)
    top_vals, top_ids, top_hot = [], [], []
    for _ in range(TOP_K):
        m = jnp.max(work, axis=-1, keepdims=True)
        first = jnp.min(jnp.where(work == m, lane, float(LANES)), axis=-1, keepdims=True)
        hot = lane == first
        top_vals.append(m)
        top_ids.append(first)
        top_hot.append(hot)
        work = jnp.where(hot, NEG_INF, work)
    exps = [jnp.exp(v - top_vals[0]) for v in top_vals]
    den = exps[0]
    for e in exps[1:]:
        den = den + e

    @pl.when(pl.program_id(0) == 0)
    def _():
        cnt_ref[...] = cnt_in_ref[...]

    picks = jnp.zeros((tm, LANES), F32)
    for hot in top_hot:
        picks = picks + jnp.where(hot, 1.0, 0.0)
    earlier = lax.broadcasted_iota(I32, (tm, tm), 1) < lax.broadcasted_iota(I32, (tm, tm), 0)
    before = _dot(jnp.where(earlier, 1.0, 0.0).astype(BF16), picks.astype(BF16)) + cnt_ref[...]
    cnt_ref[...] += jnp.sum(picks, axis=0, keepdims=True)
    cnt_out_ref[...] = cnt_ref[...]

    route = jnp.zeros((tm, LANES), F32)
    for k in range(TOP_K):
        rank = jnp.sum(jnp.where(top_hot[k], before, 0.0), axis=-1, keepdims=True)
        route = jnp.where(lane == k, top_ids[k], route)
        route = jnp.where(lane == TOP_K + k, exps[k] / den, route)
        route = jnp.where(lane == 2 * TOP_K + k, rank, route)
    route_ref[...] = route


def _outproj(x2d, ret, att, wo, g1, sc2, sh2, n2, rw, rb, cnt_in, *, tm, rows_per_batch, per_row,
             n_experts, precise):
    n, d = x2d.shape
    half = ret.shape[1]
    row = lambda i: (i, 0)
    const = lambda i: (0, 0)
    if per_row:
        mod_spec = pl.BlockSpec((tm, d), row)
    else:
        tiles_per_batch = rows_per_batch // tm
        mod_spec = pl.BlockSpec((None, 1, d), lambda i: (i // tiles_per_batch, 0, 0))
    return pl.pallas_call(
        functools.partial(_outproj_body, n_experts=n_experts, precise=precise), grid=(n // tm,),
        in_specs=[pl.BlockSpec((tm, d), row), pl.BlockSpec((tm, half), row), pl.BlockSpec((tm, half), row),
                  pl.BlockSpec(wo.shape, const), mod_spec, mod_spec, mod_spec, pl.BlockSpec((1, d), const),
                  pl.BlockSpec(rw.shape, const), pl.BlockSpec((1, LANES), const),
                  pl.BlockSpec((1, LANES), const)],
        out_specs=[pl.BlockSpec((tm, d), row), pl.BlockSpec((tm, d // LANES, LANES), lambda i: (i, 0, 0)),
                   pl.BlockSpec((tm, LANES), row),
                   pl.BlockSpec((1, LANES), const)],
        out_shape=[jax.ShapeDtypeStruct((n, d), F32), jax.ShapeDtypeStruct((n, d // LANES, LANES), F32),
                   jax.ShapeDtypeStruct((n, LANES), F32), jax.ShapeDtypeStruct((1, LANES), F32)],
        scratch_shapes=[pltpu.VMEM((1, LANES), F32)],
        compiler_params=_cparams(("arbitrary",)), name="outproj",
    )(x2d, ret, att, wo, g1, sc2, sh2, n2, rw, rb, cnt_in)


EXPERT_TILE = 512


def _store_row_tiles(ref, x):
    for c in range(ref.shape[1]):
        ref[:, c, :] = x[:, c * LANES:(c + 1) * LANES]


def _load_row_tiles(ref):
    return jnp.concatenate([ref[:, c, :] for c in range(ref.shape[1])], axis=1)


def _experts_body(te_ref, src_ref, nused_ref, h_ref, wgu_ref, bgu_ref, wd_ref, bd_ref, y_ref,
                  xbuf_ref, sem_ref, wgu_bf_ref, wd_bf_ref, *, f_chunk):
    t = pl.program_id(0)
    n_tiles = pl.num_programs(0)
    slot = t % 2
    n_used = nused_ref[0]
    tm = y_ref.shape[0]
    d_ff = wd_ref.shape[0]
    n_chunks = d_ff // f_chunk

    def row_copy(token, s, r):
        return pltpu.make_async_copy(h_ref.at[token], xbuf_ref.at[s, r], sem_ref.at[s])

    def start_rows(tile, s, rows):
        for r in rows:
            row_copy(src_ref[tile * tm + r], s, r).start(priority=r % 2)

    def wait_rows(s):
        for r in range(tm):
            row_copy(0, s, r).wait()

    @pl.when((t == 0) & (n_used > 0))
    def _():
        start_rows(0, 0, range(tm))

    @pl.when((t == n_used) & (t > 0))
    def _():
        wait_rows(slot)

    @pl.when(t >= n_used)
    def _():
        y_ref[...] = jnp.zeros(y_ref.shape, F32)

    @pl.when(t < n_used)
    def _():
        wait_rows(slot)

        @pl.when((t == 0) | (te_ref[t] != te_ref[jnp.maximum(t - 1, 0)]))
        def _():
            wgu_bf_ref[...] = wgu_ref[...].astype(BF16)
            wd_bf_ref[...] = wd_ref[...].astype(BF16)

        nxt = jnp.minimum(t + 1, n_tiles - 1)
        x = _load_row_tiles(xbuf_ref.at[slot]).astype(BF16)
        y = jnp.zeros((tm, wd_ref.shape[1]), F32)
        for j in range(n_chunks):
            cols = slice(j * f_chunk, (j + 1) * f_chunk)
            ucols = slice(d_ff + j * f_chunk, d_ff + (j + 1) * f_chunk)
            hg = _dot(x, wgu_bf_ref[:, cols]) + bgu_ref[:, cols]
            hu = _dot(x, wgu_bf_ref[:, ucols]) + bgu_ref[:, ucols]
            gate = jnp.minimum(hg, SWIGLU_LIMIT)
            up = jnp.clip(hu, -SWIGLU_LIMIT, SWIGLU_LIMIT)
            act = (up + 1.0) * gate * _sigmoid(SWIGLU_ALPHA * gate)
            y = y + _dot(act.astype(BF16), wd_bf_ref[cols, :])
            start_rows(nxt, 1 - slot, range(j * tm // n_chunks, (j + 1) * tm // n_chunks))
        _store_row_tiles(y_ref, y + bd_ref[...])

        @pl.when(t == n_tiles - 1)
        def _():
            wait_rows(1 - slot)


def _experts(tile_expert, src_token, n_used, h2, wgu, bgu, wd, bd, *, f_chunk):
    n_tiles = tile_expert.shape[0]
    _, d_tiles, _ = h2.shape
    d = d_tiles * LANES
    _, _, two_f = wgu.shape
    d_ff = two_f // 2
    by_expert = lambda r, c: pl.BlockSpec((None, r, c), lambda t, te, src, nu: (te[t], 0, 0))
    grid_spec = pltpu.PrefetchScalarGridSpec(
        num_scalar_prefetch=3, grid=(n_tiles,),
        in_specs=[pl.BlockSpec(memory_space=pl.ANY), by_expert(d, two_f), by_expert(1, two_f),
                  by_expert(d_ff, d), by_expert(1, d)],
        out_specs=pl.BlockSpec((EXPERT_TILE, d_tiles, LANES), lambda t, te, src, nu: (t, 0, 0)),
        scratch_shapes=[pltpu.VMEM((2, EXPERT_TILE, d_tiles, LANES), F32), pltpu.SemaphoreType.DMA((2,)),
                        pltpu.VMEM((d, two_f), BF16), pltpu.VMEM((d_ff, d), BF16)],
    )
    return pl.pallas_call(
        functools.partial(_experts_body, f_chunk=f_chunk), grid_spec=grid_spec,
        out_shape=jax.ShapeDtypeStruct((n_tiles * EXPERT_TILE, d_tiles, LANES), F32),
        compiler_params=_cparams(("arbitrary",)), name="experts",
    )(tile_expert, src_token, n_used, h2, wgu, bgu, wd, bd)


def _combine_body(pos_ref, y_ref, route_ref, x1_ref, g2_ref, out_ref, ybuf_ref, sem_ref, *, first_token, n_tokens):
    i = pl.program_id(0)
    slot = i % 2
    tc = out_ref.shape[0]

    def row_copy(pos, s, r, k):
        return pltpu.make_async_copy(y_ref.at[pos], ybuf_ref.at[s, k, r], sem_ref.at[s])

    def gather(tile, s):
        for k in range(TOP_K):
            for r in range(tc):
                row_copy(pos_ref[k * n_tokens + first_token + tile * tc + r], s, r, k).start(priority=r % 2)

    @pl.when(i == 0)
    def _():
        gather(0, 0)

    @pl.when(i + 1 < pl.num_programs(0))
    def _():
        gather(i + 1, 1 - slot)

    for k in range(TOP_K):
        for r in range(tc):
            row_copy(0, slot, r, k).wait()

    ff = jnp.zeros(out_ref.shape, F32)
    for k in range(TOP_K):
        ff = ff + route_ref[:, TOP_K + k:TOP_K + k + 1] * _load_row_tiles(ybuf_ref.at[slot, k])
    out_ref[...] = x1_ref[...] + g2_ref[...] * ff


def _combine(pos_flat, y, route, x1, g2, *, tc, rows_per_batch, per_row, first_token, n_tokens):
    n, d = x1.shape
    row = lambda i, pos: (i, 0)
    if per_row:
        mod_spec = pl.BlockSpec((tc, d), row)
    else:
        tiles_per_batch = rows_per_batch // tc
        mod_spec = pl.BlockSpec((None, 1, d), lambda i, pos: (i // tiles_per_batch, 0, 0))
    grid_spec = pltpu.PrefetchScalarGridSpec(
        num_scalar_prefetch=1, grid=(n // tc,),
        in_specs=[pl.BlockSpec(memory_space=pl.ANY), pl.BlockSpec((tc, LANES), row), pl.BlockSpec((tc, d), row),
                  mod_spec],
        out_specs=pl.BlockSpec((tc, d), row),
        scratch_shapes=[pltpu.VMEM((2, TOP_K, tc, d // LANES, LANES), F32), pltpu.SemaphoreType.DMA((2,))],
    )
    return pl.pallas_call(
        functools.partial(_combine_body, first_token=first_token, n_tokens=n_tokens), grid_spec=grid_spec,
        out_shape=jax.ShapeDtypeStruct((n, d), F32),
        compiler_params=_cparams(("arbitrary",)), name="combine",
    )(pos_flat, y, route, x1, g2)


def _dispatch_plan(route, counts, n_experts, n_tiles):
    n = route.shape[0]
    ids = route[:, :TOP_K].astype(I32)
    ranks = route[:, 2 * TOP_K:3 * TOP_K].astype(I32)
    cnt = counts[0, :n_experts].astype(I32)
    tiles_per_expert = (cnt + EXPERT_TILE - 1) // EXPERT_TILE
    tile_end = jnp.cumsum(tiles_per_expert)
    row_start = (tile_end - tiles_per_expert) * EXPERT_TILE
    pos = (row_start[ids] + ranks).T
    n_used = tile_end[-1]
    tile = jnp.arange(n_tiles, dtype=I32)
    last = jnp.minimum(tile, n_used - 1)
    tile_expert = jnp.sum((tile_end[None, :] <= last[:, None]).astype(I32), axis=1)
    token = jnp.broadcast_to(jnp.arange(n, dtype=I32)[None, :], pos.shape)
    src_token = jnp.zeros((n_tiles * EXPERT_TILE,), I32).at[pos.reshape(-1)].set(token.reshape(-1))
    return pos.reshape(-1), tile_expert, src_token, n_used.reshape(1).astype(I32)


def _rope_tables(pos, d):
    pos = np.asarray(pos, np.float64)
    inv = np.power(ROPE_THETA, -np.arange(0, d, 2, dtype=np.float64) / d)
    ang = pos[:, None] * inv[None, :]
    cos = np.cos(np.concatenate([ang, ang], axis=-1))
    sin = np.sin(ang)
    sin = np.concatenate([-sin, sin], axis=-1)
    reps = LANES // d
    return jnp.asarray(np.tile(cos, (1, reps)), F32), jnp.asarray(np.tile(sin, (1, reps)), F32)


def _split_bf16(x):
    hi = x.astype(BF16)
    return hi, (x - hi.astype(F32)).astype(BF16)


def _adaln_body(c_ref, w_ref, b_ref, o_ref):
    c = c_ref[...]
    a_hi, a_lo = _split_bf16(c * _sigmoid(c))
    w_hi, w_lo = _split_bf16(w_ref[...])
    o_ref[...] = _dot(a_hi, w_hi) + _dot(a_lo, w_hi) + _dot(a_hi, w_lo) + b_ref[...]


def _adaln(c, ada_w, ada_b):
    n, d = c.shape
    n_out = ada_w.shape[1]
    tn = d
    out = pl.pallas_call(
        _adaln_body, grid=(n_out // tn,),
        in_specs=[pl.BlockSpec((n, d), lambda j: (0, 0)), pl.BlockSpec((d, tn), lambda j: (0, j)),
                  pl.BlockSpec((1, tn), lambda j: (0, j))],
        out_specs=pl.BlockSpec((n, tn), lambda j: (0, j)),
        out_shape=jax.ShapeDtypeStruct((n, n_out), F32),
        compiler_params=_cparams(("arbitrary",)), name="adaln",
    )(c, ada_w, ada_b[None])
    return jnp.split(out, 6, axis=-1)


def kernel(x_prompt, x_sample, cache_k, cache_v, cache_idx_k, state_ret, page_table, c_prompt, c_sample,
           ada_w, ada_b, norm1_w, w_in, q_norm_w, k_norm_w, idx_k_norm_w, ret_gn_w, w_out, norm2_w,
           router_w, router_b, w_gate_up, b_gate_up, w_down, b_down):
    depth = w_in.shape[0]
    b, t, d = x_prompt.shape
    db, ts, _ = x_sample.shape
    assert ts == 1, "the sample group decodes one token per sequence"
    _, _, h_ret, d_ret, _ = state_ret.shape
    _, n_pool, page, h_att, d_head = cache_k.shape
    assert page == PAGE_SIZE
    d_idx = cache_idx_k.shape[-1]
    ret_w, att_w = h_ret * d_ret, h_att * d_head
    assert ret_w == att_w == H_IDX * d_idx and d_ret == d_head == d_idx == 64
    n_experts = router_w.shape[-1]
    n_pages = page_table.shape[1]
    past = n_pages * PAGE_SIZE
    n_main = 8 * ret_w

    cos_p, sin_p = _rope_tables(np.arange(t), d_ret)
    cos_s, sin_s = _rope_tables(np.full((db,), past), d_ret)
    log_g = jnp.log1p(-jnp.power(2.0, -5.0 - jnp.arange(h_ret, dtype=F32)))
    gmat = jnp.asarray(np.kron(np.eye(ret_w // d_head), np.full((d_head, d_head), 1.0 / d_head)), BF16)

    hp, hs = x_prompt.reshape(b * t, d), x_sample.reshape(db, d)
    outs = [[] for _ in range(8)]
    for l in range(depth):
        w_main = w_in[l, :, :n_main]
        w_tail = jnp.pad(w_in[l, :, n_main:], ((0, 0), (0, LANES - (w_in.shape[2] - n_main))))
        qn = jnp.tile(q_norm_w[l], h_att)[None]
        kn = jnp.tile(k_norm_w[l], h_att)[None]
        ikn = jnp.pad(idx_k_norm_w[l], (0, LANES - d_idx))[None]
        gnw = ret_gn_w[l].reshape(h_ret, 1, d_ret)
        wo = w_out[l]
        rw = jnp.pad(router_w[l], ((0, 0), (0, LANES - n_experts)))
        rb = jnp.pad(router_b[l], (0, LANES - n_experts))[None]
        n1, n2 = norm1_w[l][None], norm2_w[l][None]
        bgu, bd = b_gate_up[l][:, None, :], b_down[l][:, None, :]
        inproj = functools.partial(_inproj, n1=n1, qn=qn, kn=kn, ikn=ikn, gmat=gmat,
                                   d_ret=d_ret, d_head=d_head, d_idx=d_idx)
        outproj = functools.partial(_outproj, n2=n2, rw=rw, rb=rb, n_experts=n_experts)

        mods = _adaln(jnp.concatenate([c_prompt, c_sample], axis=0), ada_w[l], ada_b[l])

        sh1, sc1, g1, sh2, sc2, g2 = [a[:b, None, :] for a in mods]
        rq, rk, rv, rg, aq, ak, akb, av, avb, iq, tail = inproj(
            hp, sc1, sh1, w_main=w_main.astype(BF16), w_tail=w_tail.astype(BF16), cos=cos_p, sin=sin_p,
            tm=256, rows_per_batch=t, per_row=False, precise=False)

        heads = lambda a, nh: a.reshape(b, t, nh, -1).transpose(0, 2, 1, 3)
        ret_o, ret_s = _ret_prompt(log_g, heads(rq, h_ret), heads(rk, h_ret).swapaxes(2, 3), heads(rv, h_ret),
                                   heads(rg, h_ret), gnw, chunk=256)
        ret_o = ret_o.transpose(0, 2, 1, 3).reshape(b * t, ret_w)

        tail3 = tail.reshape(b, t, LANES)
        ikp = tail3[:, :, :d_idx]
        seq_t = lambda a: a.reshape(b, t, -1).swapaxes(1, 2)
        iwt = tail3[:, :, d_idx:d_idx + H_IDX].swapaxes(1, 2) * (d_idx ** -0.5)
        att_t = _dsa_prompt(ikp.astype(BF16), seq_t(iq), iwt, seq_t(aq), heads(akb, h_att), seq_t(avb),
                            qb=128, k_top=min(TOPK_MAX, t // 4), n_splits=4)
        att_o = att_t.swapaxes(1, 2).reshape(b * t, att_w)

        x1_p, h2_p, route_p, counts = outproj(hp, ret_o, att_o, wo=wo.astype(BF16), g1=g1, sc2=sc2, sh2=sh2,
                                              cnt_in=jnp.zeros((1, LANES), F32), tm=512, rows_per_batch=t,
                                              per_row=False, precise=False)
        g2_p = g2
        outs[0].append(ak.reshape(b, t, h_att, d_head))
        outs[1].append(av.reshape(b, t, h_att, d_head))
        outs[2].append(ikp)
        outs[3].append(ret_s)

        sh1, sc1, g1, sh2, sc2, g2 = [a[b:] for a in mods]
        rq, rk, rv, rg, aq, ak, akb, av, avb, iq, tail = inproj(
            hs, sc1, sh1, w_main=w_main, w_tail=w_tail, cos=cos_s, sin=sin_s, tm=db, rows_per_batch=1,
            per_row=True, precise=True)
        hd = lambda a: a.reshape(db, h_ret, d_ret)
        ret_o, ret_s = _ret_sample(log_g, hd(rq), hd(rk), hd(rv), hd(rg), state_ret[l], gnw)

        iks = tail[:, :d_idx]
        iw = tail[:, d_idx:d_idx + H_IDX] * (d_idx ** -0.5)
        scores = _sample_index_scores(page_table, iq.reshape(db, H_IDX, d_idx), iw[:, :, None], iks[:, :, None],
                                      cache_idx_k[l].transpose(0, 2, 1))
        bias = _sample_select(scores, n_visible=past + ts, k_top=min(TOPK_MAX, (past + ts) // 4))
        hda = lambda a: a.reshape(db, h_att, d_head, 1)
        att_o = _sample_attention(page_table, hda(aq), bias, hda(ak), hda(av), cache_k[l].transpose(0, 2, 3, 1),
                                  cache_v[l].transpose(0, 2, 3, 1), group=8)

        x1_s, h2_s, route_s, counts = outproj(hs, ret_o, att_o.reshape(db, att_w), wo=wo, g1=g1, sc2=sc2, sh2=sh2,
                                              cnt_in=counts, tm=db, rows_per_batch=1, per_row=True, precise=True)

        n_all = b * t + db
        n_tiles = -(-TOP_K * n_all // EXPERT_TILE) + n_experts
        pos, tile_expert, src_token, n_used = _dispatch_plan(
            jnp.concatenate([route_p, route_s], axis=0), counts, n_experts, n_tiles)
        y = _experts(tile_expert, src_token, n_used, jnp.concatenate([h2_p, h2_s], axis=0),
                     w_gate_up[l], bgu, w_down[l], bd, f_chunk=512)
        hp = _combine(pos, y, route_p, x1_p, g2_p, tc=128, rows_per_batch=t, per_row=False, first_token=0,
                      n_tokens=n_all)
        hs = _combine(pos, y, route_s, x1_s, g2, tc=db, rows_per_batch=1, per_row=True, first_token=b * t,
                      n_tokens=n_all)
        outs[4].append(ak.reshape(db, ts, h_att, d_head))
        outs[5].append(av.reshape(db, ts, h_att, d_head))
        outs[6].append(iks.reshape(db, ts, d_idx))
        outs[7].append(ret_s)

    k_p, v_p, ik_p, r_p, k_s, v_s, ik_s, r_s = [jnp.stack(o) for o in outs]
    return (hp.reshape(b, t, d), hs.reshape(db, ts, d), k_p, v_p, ik_p, r_p, k_s, v_s, ik_s, r_s)
```

```python
import functools
import math

import numpy as np
import jax
import jax.numpy as jnp
from jax import lax
from jax.experimental import pallas as pl
from jax.experimental.pallas import tpu as pltpu

F32 = jnp.float32
BF16 = jnp.bfloat16
I32 = jnp.int32

PAGE_SIZE = 128
H_IDX = 8
TOPK_MAX = 256
TOP_K = 4
SWIGLU_LIMIT = 7.0
SWIGLU_ALPHA = 1.702
ROPE_THETA = 10000.0
EPS = 1e-6
INT_MIN = -(2 ** 31)
NEG_INF = float("-inf")

LANES = 128
VMEM_LIMIT = 56 * 1024 * 1024


def _cparams(sem):
    return pltpu.CompilerParams(dimension_semantics=sem, vmem_limit_bytes=VMEM_LIMIT)


def _dot(a, b):
    return jnp.dot(a, b, preferred_element_type=F32)


def _dot_nt(a, b):
    return lax.dot_general(a, b, (((1,), (1,)), ((), ())), preferred_element_type=F32)


def _sigmoid(x):
    return 1.0 / (1.0 + jnp.exp(-x))


def _dot_hp(a, b):
    return jnp.dot(a, b, preferred_element_type=F32, precision=lax.Precision.HIGHEST)


def _dot_nt_hp(a, b):
    return lax.dot_general(a, b, (((1,), (1,)), ((), ())), preferred_element_type=F32,
                           precision=lax.Precision.HIGHEST)


def _ordered_float(key):
    bits = jnp.where(key < 0, key ^ jnp.int32(0x7FFFFFFF), key)
    return pltpu.bitcast(bits, F32)


def _rope_halves(v, cos, sin_signed, first_half, width):
    swapped = jnp.where(first_half, pltpu.roll(v, width - 32, 1), pltpu.roll(v, 32, 1))
    return v * cos + swapped * sin_signed


def _inproj_body(x_ref, sc_ref, sh_ref, n1_ref, w_ref, wt_ref, cos_ref, sin_ref, qn_ref, kn_ref,
                 ikn_ref, gmat_ref,
                 rq_ref, rk_ref, rv_ref, rg_ref, aq_ref, ak_ref, akb_ref, av_ref, avb_ref, iq_ref,
                 tail_ref, *, seg_w, d_idx, iw_scale, k_scale, q_scale, precise):
    x = x_ref[...]
    tm = x.shape[0]
    ms = jnp.mean(x * x, axis=-1, keepdims=True)
    xn = x * lax.rsqrt(ms + EPS) * n1_ref[...]
    h = xn * (1.0 + sc_ref[...]) + sh_ref[...]
    if precise:
        mm = _dot_hp
    else:
        mm = _dot
        h = h.astype(BF16)

    cos1 = cos_ref[...]
    sin1 = sin_ref[...]
    reps = seg_w // LANES
    cos = jnp.concatenate([cos1] * reps, axis=1)
    sin = jnp.concatenate([sin1] * reps, axis=1)
    lane = lax.broadcasted_iota(I32, (tm, seg_w), 1)
    first = (lane & 32) == 0
    rope = functools.partial(_rope_halves, cos=cos, sin_signed=sin, first_half=first, width=seg_w)

    def seg(j):
        return mm(h, w_ref[:, j * seg_w:(j + 1) * seg_w])

    def head_norm(v, w):
        sq = v * v
        hi = sq.astype(BF16)
        lo = (sq - hi.astype(F32)).astype(BF16)
        msq = _dot(hi, gmat_ref[...]) + _dot(lo, gmat_ref[...])
        return v * lax.rsqrt(msq + EPS) * w

    rq_ref[...] = rope(seg(0)).astype(rq_ref.dtype)
    rk_ref[...] = (rope(seg(1)) * k_scale).astype(rk_ref.dtype)
    rv_ref[...] = seg(2).astype(rv_ref.dtype)
    rg_ref[...] = seg(3).astype(rg_ref.dtype)
    aq_ref[...] = (rope(head_norm(seg(4), qn_ref[...])) * q_scale).astype(aq_ref.dtype)
    ak = rope(head_norm(seg(5), kn_ref[...]))
    ak_ref[...] = ak
    akb_ref[...] = ak.astype(BF16)
    av = seg(6)
    av_ref[...] = av
    avb_ref[...] = av.astype(BF16)
    iq_ref[...] = rope(seg(7)).astype(iq_ref.dtype)

    zt = mm(h, wt_ref[...])
    lane_t = lax.broadcasted_iota(I32, (tm, LANES), 1)
    is_key = lane_t < d_idx
    msk = jnp.sum(jnp.where(is_key, zt * zt, 0.0), axis=-1, keepdims=True) * (1.0 / d_idx)
    kn = zt * lax.rsqrt(msk + EPS) * ikn_ref[...]
    kr = _rope_halves(kn, cos1, sin1, (lane_t & 32) == 0, LANES)
    tail_ref[...] = jnp.where(is_key, kr, zt * iw_scale)


def _inproj(x2d, sc, sh, n1, w_main, w_tail, cos, sin, qn, kn, ikn, gmat, *, tm, rows_per_batch,
            per_row, precise, d_ret, d_head, d_idx):
    act_dtype = F32 if precise else BF16
    n, d = x2d.shape
    seg_w = w_main.shape[1] // 8
    grid = (n // tm,)
    row = lambda i: (i, 0)
    const = lambda i: (0, 0)
    if per_row:
        mod_spec = pl.BlockSpec((tm, d), row)
        pos_spec = pl.BlockSpec((tm, LANES), row)
    else:
        tiles_per_batch = rows_per_batch // tm
        mod_spec = pl.BlockSpec((None, 1, d), lambda i: (i // tiles_per_batch, 0, 0))
        pos_spec = pl.BlockSpec((tm, LANES), lambda i: (i % tiles_per_batch, 0))
    in_specs = [
        pl.BlockSpec((tm, d), row), mod_spec, mod_spec, pl.BlockSpec((1, d), const),
        pl.BlockSpec(w_main.shape, const), pl.BlockSpec(w_tail.shape, const),
        pos_spec, pos_spec,
        pl.BlockSpec((1, seg_w), const), pl.BlockSpec((1, seg_w), const), pl.BlockSpec((1, LANES), const),
        pl.BlockSpec(gmat.shape, const),
    ]
    seg_spec = pl.BlockSpec((tm, seg_w), row)
    out_dtypes = [act_dtype, act_dtype, act_dtype, act_dtype, act_dtype, F32, BF16, F32, BF16, act_dtype]
    out_shape = [jax.ShapeDtypeStruct((n, seg_w), dt) for dt in out_dtypes]
    out_shape.append(jax.ShapeDtypeStruct((n, LANES), F32))
    out_specs = [seg_spec] * 10 + [pl.BlockSpec((tm, LANES), row)]
    body = functools.partial(_inproj_body, seg_w=seg_w, d_idx=d_idx, iw_scale=H_IDX ** -0.5,
                             k_scale=d_ret ** -0.5, q_scale=d_head ** -0.5, precise=precise)
    return pl.pallas_call(
        body, grid=grid, in_specs=in_specs, out_specs=out_specs, out_shape=out_shape,
        compiler_params=_cparams(("arbitrary",)), name="inproj",
    )(x2d, sc, sh, n1, w_main, w_tail, cos, sin, qn, kn, ikn, gmat)


def _group_norm_gate(o, g, gnw):
    mu = jnp.mean(o, axis=-1, keepdims=True)
    var = jnp.mean((o - mu) ** 2, axis=-1, keepdims=True)
    on = (o - mu) * lax.rsqrt(var + EPS) * gnw
    g = g.astype(F32)
    return g * _sigmoid(g) * on


def _ret_prompt_body(lg_ref, q_ref, kt_ref, v_ref, g_ref, gnw_ref, o_ref, s_ref, *, chunk):
    t, dk = q_ref.shape
    lg = lg_ref[pl.program_id(1)]
    ii = lax.broadcasted_iota(I32, (chunk, chunk), 0)
    jj = lax.broadcasted_iota(I32, (chunk, chunk), 1)
    rel = (ii - jj).astype(F32)
    decay = jnp.where(rel >= 0.0, jnp.exp(jnp.maximum(rel, 0.0) * lg), 0.0)
    q_dec = jnp.exp((lax.broadcasted_iota(I32, (chunk, 1), 0).astype(F32) + 1.0) * lg)
    k_dec = jnp.exp((chunk - 1.0 - lax.broadcasted_iota(I32, (1, chunk), 1).astype(F32)) * lg)
    g_chunk = jnp.exp(jnp.full((1, 1), chunk, F32) * lg)
    gnw = gnw_ref[...]
    state = jnp.zeros((dk, v_ref.shape[1]), F32)
    for c in range(t // chunk):
        rows = slice(c * chunk, (c + 1) * chunk)
        qc = q_ref[rows, :]
        ktc = kt_ref[:, rows]
        vc = v_ref[rows, :]
        inner = _dot(qc, ktc) * decay
        o = _dot(inner.astype(BF16), vc) + _dot(qc, state.astype(BF16)) * q_dec
        state = state * g_chunk + _dot((ktc.astype(F32) * k_dec).astype(BF16), vc)
        o_ref[rows, :] = _group_norm_gate(o, g_ref[rows, :], gnw).astype(o_ref.dtype)
    s_ref[...] = state


def _ret_prompt(log_g, q, kt, v, g, gnw, *, chunk):
    b, h, t, dk = q.shape
    dv = v.shape[-1]
    blk = lambda r, c: pl.BlockSpec((None, None, r, c), lambda i, j, lg: (i, j, 0, 0))
    grid_spec = pltpu.PrefetchScalarGridSpec(
        num_scalar_prefetch=1, grid=(b, h),
        in_specs=[blk(t, dk), blk(dk, t), blk(t, dv), blk(t, dv),
                  pl.BlockSpec((None, 1, dv), lambda i, j, lg: (j, 0, 0))],
        out_specs=[blk(t, dv), blk(dk, dv)],
    )
    return pl.pallas_call(
        functools.partial(_ret_prompt_body, chunk=chunk), grid_spec=grid_spec,
        out_shape=[jax.ShapeDtypeStruct((b, h, t, dv), BF16), jax.ShapeDtypeStruct((b, h, dk, dv), F32)],
        compiler_params=_cparams(("arbitrary", "arbitrary")), name="ret_prompt",
    )(log_g, q, kt, v, g, gnw)


def _ret_sample_body(lg_ref, qc_ref, kc_ref, v_ref, g_ref, s0_ref, gnw_ref, o_ref, s_ref):
    for h in range(qc_ref.shape[0]):
        gamma = jnp.exp(jnp.full((1, 1), 1.0, F32) * lg_ref[h])
        qc = qc_ref[h]
        kc = kc_ref[h]
        v = v_ref[h]
        s0 = s0_ref[h]
        qk = jnp.sum(qc * kc, axis=0, keepdims=True)
        o = qk * v + jnp.sum(qc * s0, axis=0, keepdims=True) * gamma
        s_ref[h] = s0 * gamma + kc * v
        o_ref[h] = _group_norm_gate(o, g_ref[h], gnw_ref[h]).astype(o_ref.dtype)


def _ret_sample(log_g, q, k, v, g, s0, gnw):
    db, h, dk = q.shape
    dv = v.shape[-1]
    col = pl.BlockSpec((None, h, dk, 1), lambda i, lg: (i, 0, 0, 0))
    rowv = pl.BlockSpec((None, h, 1, dv), lambda i, lg: (i, 0, 0, 0))
    mat = pl.BlockSpec((None, h, dk, dv), lambda i, lg: (i, 0, 0, 0))
    grid_spec = pltpu.PrefetchScalarGridSpec(
        num_scalar_prefetch=1, grid=(db,),
        in_specs=[col, col, rowv, rowv, mat, pl.BlockSpec((h, 1, dv), lambda i, lg: (0, 0, 0))],
        out_specs=[rowv, mat],
    )
    o, s = pl.pallas_call(
        _ret_sample_body, grid_spec=grid_spec,
        out_shape=[jax.ShapeDtypeStruct((db, h, 1, dv), F32), jax.ShapeDtypeStruct((db, h, dk, dv), F32)],
        compiler_params=_cparams(("arbitrary",)), name="ret_sample",
    )(log_g, q.reshape(db, h, dk, 1), k.reshape(db, h, dk, 1), v.reshape(db, h, 1, dv),
      g.reshape(db, h, 1, dv), s0, gnw)
    return o.reshape(db, h * dv), s


ROW_CHAINS = 64
KEY_CHUNK = 256


def _reduce_rows(x, combine, finish):
    rows = x.shape[0]
    width = math.gcd(rows, ROW_CHAINS)
    acc = x[0:width]
    for c in range(1, rows // width):
        acc = combine(acc, x[c * width:(c + 1) * width])
    return finish(acc, axis=0, keepdims=True)


def _select_topk(sc_ref, pos, n_visible, k_top, n_pos_bits, axis):
    def count(mask):
        ones = jnp.where(mask, 1.0, 0.0)
        return _reduce_rows(ones, jnp.add, jnp.sum) if axis == 0 else jnp.sum(ones, axis=axis, keepdims=True)

    kf = float(k_top)
    c0 = count(sc_ref[...] >= 0.0)
    cur = jnp.where(c0 >= kf, jnp.int32(0), jnp.int32(INT_MIN))

    def value_step(i, cur):
        cand = cur + (jnp.int32(1) << (30 - i))
        return jnp.where(count(sc_ref[...] >= _ordered_float(cand)) >= kf, cand, cur)

    thr = _ordered_float(lax.fori_loop(0, 31, value_step, cur))
    need = kf - count(sc_ref[...] > thr)

    def tie_step(i, m):
        cand = m + (jnp.int32(1) << (n_pos_bits - 1 - i))
        below = count((sc_ref[...] == thr) & (pos < cand))
        return jnp.where(below < need, cand, m)

    surplus = jnp.max(count(sc_ref[...] == thr) - need)
    cut = lax.cond(surplus > 0.0,
                   lambda: lax.fori_loop(0, n_pos_bits, tie_step, jnp.zeros(thr.shape, I32)),
                   lambda: jnp.full(thr.shape, (1 << n_pos_bits) - 1, I32))
    sc = sc_ref[...]
    picked = (sc > thr) | ((sc == thr) & (pos <= cut))
    take_all = jnp.broadcast_to(n_visible, sc.shape) <= k_top
    return (take_all & (sc > NEG_INF)) | (~take_all & picked)


def _dsa_prompt_body(ik_ref, iqt_ref, iwt_ref, aqt_ref, k_ref, vt_ref, ot_ref, sc_ref, bias_ref,
                     *, k_top, d_idx, d_head, first_qblock):
    t, qb = sc_ref.shape
    j = pl.program_id(1) + first_qblock
    krow = lax.broadcasted_iota(I32, (t, qb), 0)
    qpos = j * qb + lax.broadcasted_iota(I32, (1, qb), 1)

    kc = math.gcd(t, KEY_CHUNK)
    chunks = [slice(c * kc, (c + 1) * kc) for c in range(t // kc)]

    for rows in chunks:
        ik = ik_ref[rows, :]
        sc = jnp.zeros((kc, qb), F32)
        for h in range(iwt_ref.shape[0]):
            s = _dot(ik, iqt_ref[h * d_idx:(h + 1) * d_idx, :])
            sc = sc + jnp.maximum(s, 0.0) * iwt_ref[h:h + 1, :]
        sc_ref[rows, :] = jnp.where(krow[rows, :] <= qpos, sc, NEG_INF)
    sel = _select_topk(sc_ref, krow, qpos + 1, k_top, int(math.ceil(math.log2(t))), 0)
    bias_ref[...] = jnp.where(sel, 0.0, NEG_INF)

    for h in range(k_ref.shape[0]):
        hrows = slice(h * d_head, (h + 1) * d_head)
        s = _dot(k_ref[h], aqt_ref[hrows, :]) + bias_ref[...]
        m = _reduce_rows(s, jnp.maximum, jnp.max)
        p = jnp.exp(s - m)
        l = _reduce_rows(p, jnp.add, jnp.sum)
        o = _dot(vt_ref[hrows, :], p.astype(BF16))
        ot_ref[hrows, :] = (o / l).astype(ot_ref.dtype)


def _dsa_prompt(ik, iqt, iwt, aqt, kh, vt, *, qb, k_top, n_splits):
    b, t, d_idx = ik.shape
    n_heads, d_head = kh.shape[1], kh.shape[3]
    w = vt.shape[1]
    nq = t // qb // n_splits
    outs = []
    for s in range(n_splits):
        tk = (s + 1) * t // n_splits
        per_b = lambda r, c: pl.BlockSpec((None, r, c), lambda i, j: (i, 0, 0))
        per_q = lambda r, s=s: pl.BlockSpec((None, r, qb), lambda i, j: (i, 0, j + s * nq))
        outs.append(pl.pallas_call(
            functools.partial(_dsa_prompt_body, k_top=k_top, d_idx=d_idx, d_head=d_head, first_qblock=s * nq),
            grid=(b, nq),
            in_specs=[per_b(tk, d_idx), per_q(iqt.shape[1]), per_q(iwt.shape[1]), per_q(w),
                      pl.BlockSpec((None, n_heads, tk, d_head), lambda i, j: (i, 0, 0, 0)), per_b(w, tk)],
            out_specs=pl.BlockSpec((None, w, qb), lambda i, j: (i, 0, j)),
            out_shape=jax.ShapeDtypeStruct((b, w, nq * qb), BF16),
            scratch_shapes=[pltpu.VMEM((tk, qb), F32), pltpu.VMEM((tk, qb), F32)],
            compiler_params=_cparams(("arbitrary", "arbitrary")), name=f"dsa_prompt_{s}",
        )(ik, iqt, iwt, aqt, kh, vt))
    return jnp.concatenate(outs, axis=2)


def _sidx_body(pt_ref, qi_ref, iw_ref, iknew_ref, cache_ref, out_ref, buf_ref, sem_ref, *, n_pages):
    b = pl.program_id(0)
    nb = pl.num_programs(0)
    slot = b % 2
    past = n_pages * PAGE_SIZE

    def copies(seq, s):
        return [pltpu.make_async_copy(cache_ref.at[pt_ref[seq, p]],
                                      buf_ref.at[s, :, pl.ds(p * PAGE_SIZE, PAGE_SIZE)], sem_ref.at[s])
                for p in range(n_pages)]

    @pl.when(b == 0)
    def _():
        for c in copies(0, 0):
            c.start()

    @pl.when(b + 1 < nb)
    def _():
        for c in copies(b + 1, 1 - slot):
            c.start()

    for c in copies(b, slot):
        c.wait()
    lane = lax.broadcasted_iota(I32, (buf_ref.shape[1], LANES), 1)
    buf_ref[slot, :, past:] = jnp.where(lane == 0, iknew_ref[...], 0.0)

    s = _dot_hp(qi_ref[...], buf_ref[slot])
    out_ref[...] = jnp.sum(jnp.maximum(s, 0.0) * iw_ref[...], axis=0, keepdims=True)


def _sample_index_scores(page_table, qi, iw, ik_new, cache_t):
    db, n_pages = page_table.shape
    d_idx = cache_t.shape[1]
    n_keys = n_pages * PAGE_SIZE + LANES
    grid_spec = pltpu.PrefetchScalarGridSpec(
        num_scalar_prefetch=1, grid=(db,),
        in_specs=[pl.BlockSpec((None, H_IDX, d_idx), lambda i, pt: (i, 0, 0)),
                  pl.BlockSpec((None, H_IDX, 1), lambda i, pt: (i, 0, 0)),
                  pl.BlockSpec((None, d_idx, 1), lambda i, pt: (i, 0, 0)),
                  pl.BlockSpec(memory_space=pl.ANY)],
        out_specs=pl.BlockSpec((None, 1, n_keys), lambda i, pt: (i, 0, 0)),
        scratch_shapes=[pltpu.VMEM((2, d_idx, n_keys), F32), pltpu.SemaphoreType.DMA((2,))],
    )
    out = pl.pallas_call(
        functools.partial(_sidx_body, n_pages=n_pages), grid_spec=grid_spec,
        out_shape=jax.ShapeDtypeStruct((db, 1, n_keys), F32),
        compiler_params=_cparams(("arbitrary",)), name="sample_index_scores",
    )(page_table, qi, iw, ik_new, cache_t)
    return out.reshape(db, n_keys)


def _ssel_body(sc_ref, bias_ref, work_ref, *, n_visible, k_top):
    n, w = sc_ref.shape
    pos = lax.broadcasted_iota(I32, (n, w), 1)
    work_ref[...] = jnp.where(pos < n_visible, sc_ref[...], NEG_INF)
    sel = _select_topk(work_ref, pos, jnp.int32(n_visible), k_top, int(math.ceil(math.log2(w))), 1)
    bias_ref[...] = jnp.where(sel, 0.0, NEG_INF)


def _sample_select(scores, *, n_visible, k_top):
    return pl.pallas_call(
        functools.partial(_ssel_body, n_visible=n_visible, k_top=k_top),
        out_shape=jax.ShapeDtypeStruct(scores.shape, F32),
        scratch_shapes=[pltpu.VMEM(scores.shape, F32)],
        compiler_params=pltpu.CompilerParams(vmem_limit_bytes=VMEM_LIMIT), name="sample_select",
    )(scores)


def _sattn_body(pt_ref, q_ref, bias_ref, biasnew_ref, knew_ref, vnew_ref, ck_ref, cv_ref, o_ref,
                kbuf_ref, vbuf_ref, ksem_ref, vsem_ref, m_ref, l_ref, acc_ref, *, group):
    b = pl.program_id(0)
    g = pl.program_id(1)
    ng = pl.num_programs(1)
    step = b * ng + g
    total = pl.num_programs(0) * ng
    slot = step % 2
    n_heads, d_head, _ = q_ref.shape

    def copies(seq, grp, s):
        out = []
        for p in range(group):
            page = pt_ref[seq, grp * group + p]
            out.append(pltpu.make_async_copy(ck_ref.at[page], kbuf_ref.at[s, p], ksem_ref.at[s]))
            out.append(pltpu.make_async_copy(cv_ref.at[page], vbuf_ref.at[s, p], vsem_ref.at[s]))
        return out

    @pl.when(step == 0)
    def _():
        for c in copies(0, 0, 0):
            c.start()

    @pl.when(step + 1 < total)
    def _():
        nxt = step + 1
        for c in copies(nxt // ng, nxt % ng, 1 - slot):
            c.start()

    for c in copies(b, g, slot):
        c.wait()

    @pl.when(g == 0)
    def _():
        m_ref[...] = jnp.full(m_ref.shape, NEG_INF, F32)
        l_ref[...] = jnp.zeros(l_ref.shape, F32)
        acc_ref[...] = jnp.zeros(acc_ref.shape, F32)

    def online_update(h, scores, values):
        top = scores[0]
        for s in scores[1:]:
            top = jnp.maximum(top, s)
        m_old = m_ref[h]
        m_new = jnp.maximum(m_old, top)
        m_safe = jnp.where(m_new == NEG_INF, 0.0, m_new)
        alpha = jnp.exp(m_old - m_safe)
        acc = alpha * acc_ref[h]
        l = alpha * l_ref[h]
        for s, v in zip(scores, values):
            p = jnp.exp(s - m_safe)
            l = l + p
            acc = acc + p * v
        m_ref[h] = m_new
        l_ref[h] = l
        acc_ref[h] = acc

    def finish(h):
        m = m_ref[h]
        m_all = jnp.max(m, axis=-1, keepdims=True)
        w = jnp.exp(m - m_all)
        l_all = jnp.sum(l_ref[h] * w, axis=-1, keepdims=True)
        o_ref[h] = jnp.sum(acc_ref[h] * w, axis=-1, keepdims=True) / l_all

    for h in range(n_heads):
        q = q_ref[h]
        scores = [jnp.sum(kbuf_ref[slot, p, h] * q, axis=0, keepdims=True) + bias_ref[p:p + 1, :]
                  for p in range(group)]
        online_update(h, scores, [vbuf_ref[slot, p, h] for p in range(group)])

    @pl.when(g == ng - 1)
    def _():
        first_lane = lax.broadcasted_iota(I32, (1, LANES), 1) == 0
        for h in range(n_heads):
            s_new = jnp.sum(q_ref[h] * knew_ref[h], axis=0, keepdims=True) + biasnew_ref[:, 0:1]
            online_update(h, [jnp.where(first_lane, s_new, NEG_INF)], [vnew_ref[h]])
            finish(h)


def _sample_attention(page_table, q, bias, k_new, v_new, cache_kt, cache_vt, *, group):
    db, n_pages = page_table.shape
    _, n_heads, d_head, _ = q.shape
    ng = n_pages // group
    past = n_pages * PAGE_SIZE
    col = pl.BlockSpec((None, n_heads, d_head, 1), lambda i, j, pt: (i, 0, 0, 0))
    any_spec = pl.BlockSpec(memory_space=pl.ANY)
    grid_spec = pltpu.PrefetchScalarGridSpec(
        num_scalar_prefetch=1, grid=(db, ng),
        in_specs=[col,
                  pl.BlockSpec((None, None, group, PAGE_SIZE), lambda i, j, pt: (i, j, 0, 0)),
                  pl.BlockSpec((None, 1, LANES), lambda i, j, pt: (i, 0, 0)),
                  col, col, any_spec, any_spec],
        out_specs=col,
        scratch_shapes=[pltpu.VMEM((2, group, n_heads, d_head, PAGE_SIZE), F32),
                        pltpu.VMEM((2, group, n_heads, d_head, PAGE_SIZE), F32),
                        pltpu.SemaphoreType.DMA((2,)), pltpu.SemaphoreType.DMA((2,)),
                        pltpu.VMEM((n_heads, 1, LANES), F32), pltpu.VMEM((n_heads, 1, LANES), F32),
                        pltpu.VMEM((n_heads, d_head, LANES), F32)],
    )
    return pl.pallas_call(
        functools.partial(_sattn_body, group=group), grid_spec=grid_spec,
        out_shape=jax.ShapeDtypeStruct((db, n_heads, d_head, 1), F32),
        compiler_params=_cparams(("arbitrary", "arbitrary")), name="sample_attention",
    )(page_table, q, bias[:, :past].reshape(db, ng, group, PAGE_SIZE), bias[:, past:].reshape(db, 1, LANES),
      k_new, v_new, cache_kt, cache_vt)


def _outproj_body(x_ref, ret_ref, att_ref, wo_ref, g1_ref, sc2_ref, sh2_ref, n2_ref, rw_ref,
                  rb_ref, cnt_in_ref, x1_ref, h2_ref, route_ref, cnt_out_ref, cnt_ref, *, n_experts, precise):
    half = ret_ref.shape[1]
    mm = _dot_hp if precise else _dot
    y = mm(ret_ref[...], wo_ref[:half, :]) + mm(att_ref[...], wo_ref[half:, :])
    x1 = x_ref[...] + g1_ref[...] * y
    x1_ref[...] = x1
    ms = jnp.mean(x1 * x1, axis=-1, keepdims=True)
    h2 = x1 * lax.rsqrt(ms + EPS) * n2_ref[...] * (1.0 + sc2_ref[...]) + sh2_ref[...]
    _store_row_tiles(h2_ref, h2)
    hi, lo = _split_bf16(h2)
    rw_hi, rw_lo = _split_bf16(rw_ref[...])
    logits = _dot(hi, rw_hi) + _dot(lo, rw_hi) + _dot(hi, rw_lo) + rb_ref[...]

    tm = logits.shape[0]
    lane = lax.broadcasted_iota(I32, (tm, LANES), 1).astype(F32)
    work = jnp.where(lane < n_experts, logits, NEG_INF)
    top_vals, top_ids, top_hot = [], [], []
    for _ in range(TOP_K):
        m = jnp.max(work, axis=-1, keepdims=True)
        first = jnp.min(jnp.where(work == m, lane, float(LANES)), axis=-1, keepdims=True)
        hot = lane == first
        top_vals.append(m)
        top_ids.append(first)
        top_hot.append(hot)
        work = jnp.where(hot, NEG_INF, work)
    exps = [jnp.exp(v - top_vals[0]) for v in top_vals]
    den = exps[0]
    for e in exps[1:]:
        den = den + e

    @pl.when(pl.program_id(0) == 0)
    def _():
        cnt_ref[...] = cnt_in_ref[...]

    picks = jnp.zeros((tm, LANES), F32)
    for hot in top_hot:
        picks = picks + jnp.where(hot, 1.0, 0.0)
    earlier = lax.broadcasted_iota(I32, (tm, tm), 1) < lax.broadcasted_iota(I32, (tm, tm), 0)
    before = _dot(jnp.where(earlier, 1.0, 0.0).astype(BF16), picks.astype(BF16)) + cnt_ref[...]
    cnt_ref[...] += jnp.sum(picks, axis=0, keepdims=True)
    cnt_out_ref[...] = cnt_ref[...]

    route = jnp.zeros((tm, LANES), F32)
    for k in range(TOP_K):
        rank = jnp.sum(jnp.where(top_hot[k], before, 0.0), axis=-1, keepdims=True)
        route = jnp.where(lane == k, top_ids[k], route)
        route = jnp.where(lane == TOP_K + k, exps[k] / den, route)
        route = jnp.where(lane == 2 * TOP_K + k, rank, route)
    route_ref[...] = route


def _outproj(x2d, ret, att, wo, g1, sc2, sh2, n2, rw, rb, cnt_in, *, tm, rows_per_batch, per_row,
             n_experts, precise):
    n, d = x2d.shape
    half = ret.shape[1]
    row = lambda i: (i, 0)
    const = lambda i: (0, 0)
    if per_row:
        mod_spec = pl.BlockSpec((tm, d), row)
    else:
        tiles_per_batch = rows_per_batch // tm
        mod_spec = pl.BlockSpec((None, 1, d), lambda i: (i // tiles_per_batch, 0, 0))
    return pl.pallas_call(
        functools.partial(_outproj_body, n_experts=n_experts, precise=precise), grid=(n // tm,),
        in_specs=[pl.BlockSpec((tm, d), row), pl.BlockSpec((tm, half), row), pl.BlockSpec((tm, half), row),
                  pl.BlockSpec(wo.shape, const), mod_spec, mod_spec, mod_spec, pl.BlockSpec((1, d), const),
                  pl.BlockSpec(rw.shape, const), pl.BlockSpec((1, LANES), const),
                  pl.BlockSpec((1, LANES), const)],
        out_specs=[pl.BlockSpec((tm, d), row), pl.BlockSpec((tm, d // LANES, LANES), lambda i: (i, 0, 0)),
                   pl.BlockSpec((tm, LANES), row),
                   pl.BlockSpec((1, LANES), const)],
        out_shape=[jax.ShapeDtypeStruct((n, d), F32), jax.ShapeDtypeStruct((n, d // LANES, LANES), F32),
                   jax.ShapeDtypeStruct((n, LANES), F32), jax.ShapeDtypeStruct((1, LANES), F32)],
        scratch_shapes=[pltpu.VMEM((1, LANES), F32)],
        compiler_params=_cparams(("arbitrary",)), name="outproj",
    )(x2d, ret, att, wo, g1, sc2, sh2, n2, rw, rb, cnt_in)


EXPERT_TILE = 512


def _store_row_tiles(ref, x):
    for c in range(ref.shape[1]):
        ref[:, c, :] = x[:, c * LANES:(c + 1) * LANES]


def _load_row_tiles(ref):
    return jnp.concatenate([ref[:, c, :] for c in range(ref.shape[1])], axis=1)


DISPATCH_TOKENS = 128


def _dispatch_body(pos_ref, pad_lo_ref, pad_hi_ref, nused_ref, h_ref, xs_ref, sem_ref, pad_sem_ref, tile_sem_ref,
                   *, n_tokens, n_tail):
    i = pl.program_id(0)
    n_steps = pl.num_programs(0)
    td = DISPATCH_TOKENS
    n_experts = pad_lo_ref.shape[0]
    n_tiles = xs_ref.shape[0] // EXPERT_TILE

    def tile_copy(tile):
        return pltpu.make_async_copy(h_ref.at[pl.ds(0, EXPERT_TILE)],
                                     xs_ref.at[pl.ds(tile * EXPERT_TILE, EXPERT_TILE)], tile_sem_ref.at[0])

    def unused_tiles(fn):
        lax.fori_loop(nused_ref[0], n_tiles, fn, 0)

    def row_copy(token, row, sem):
        return pltpu.make_async_copy(h_ref.at[token], xs_ref.at[row], sem)

    def start_tokens(first, count, sem):
        for r in range(count):
            for k in range(TOP_K):
                row_copy(first + r, pos_ref[k * n_tokens + first + r], sem).start(priority=r % 2)

    def wait_tokens(count, sem):
        for _ in range(count * TOP_K):
            row_copy(0, 0, sem).wait()

    def pad_rows(fn):
        for e in range(n_experts):
            lax.fori_loop(pad_lo_ref[e], pad_hi_ref[e], fn, 0)

    @pl.when(i == 0)
    def _():
        def start_pad(p, carry):
            row_copy(0, p, pad_sem_ref.at[0]).start()
            return carry
        pad_rows(start_pad)

        def start_tile(tile, carry):
            tile_copy(tile).start()
            return carry
        unused_tiles(start_tile)

    start_tokens(i * td, td, sem_ref.at[i % 2])

    @pl.when(i > 0)
    def _():
        wait_tokens(td, sem_ref.at[(i - 1) % 2])

    @pl.when(i == n_steps - 1)
    def _():
        if n_tail:
            start_tokens(n_steps * td, n_tail, sem_ref.at[(i + 1) % 2])
            wait_tokens(n_tail, sem_ref.at[(i + 1) % 2])
        wait_tokens(td, sem_ref.at[i % 2])

        def wait_pad(p, carry):
            row_copy(0, p, pad_sem_ref.at[0]).wait()
            return carry
        pad_rows(wait_pad)

        def wait_tile(tile, carry):
            tile_copy(tile).wait()
            return carry
        unused_tiles(wait_tile)


def _dispatch(pos_flat, pad_lo, pad_hi, n_used, h2, n_rows):
    n_tokens, d_tiles, lanes = h2.shape
    assert n_tokens >= EXPERT_TILE
    n_steps = n_tokens // DISPATCH_TOKENS
    any_spec = pl.BlockSpec(memory_space=pl.ANY)
    grid_spec = pltpu.PrefetchScalarGridSpec(
        num_scalar_prefetch=4, grid=(n_steps,), in_specs=[any_spec], out_specs=any_spec,
        scratch_shapes=[pltpu.SemaphoreType.DMA((2,)), pltpu.SemaphoreType.DMA((1,)), pltpu.SemaphoreType.DMA((1,))],
    )
    return pl.pallas_call(
        functools.partial(_dispatch_body, n_tokens=n_tokens, n_tail=n_tokens - n_steps * DISPATCH_TOKENS),
        grid_spec=grid_spec, out_shape=jax.ShapeDtypeStruct((n_rows, d_tiles, lanes), F32),
        compiler_params=_cparams(("arbitrary",)), name="dispatch",
    )(pos_flat, pad_lo, pad_hi, n_used, h2)


def _experts_body(te_ref, nused_ref, x_ref, wgu_ref, bgu_ref, wd_ref, bd_ref, y_ref, wgu_bf_ref, wd_bf_ref,
                  *, f_chunk):
    t = pl.program_id(0)
    tm = y_ref.shape[0]
    d_ff = wd_ref.shape[0]

    @pl.when(t >= nused_ref[0])
    def _():
        y_ref[...] = jnp.zeros(y_ref.shape, F32)

    @pl.when(t < nused_ref[0])
    def _():
        @pl.when((t == 0) | (te_ref[t] != te_ref[jnp.maximum(t - 1, 0)]))
        def _():
            wgu_bf_ref[...] = wgu_ref[...].astype(BF16)
            wd_bf_ref[...] = wd_ref[...].astype(BF16)

        x = _load_row_tiles(x_ref).astype(BF16)
        y = jnp.zeros((tm, wd_ref.shape[1]), F32)
        for j in range(d_ff // f_chunk):
            cols = slice(j * f_chunk, (j + 1) * f_chunk)
            ucols = slice(d_ff + j * f_chunk, d_ff + (j + 1) * f_chunk)
            hg = _dot(x, wgu_bf_ref[:, cols]) + bgu_ref[:, cols]
            hu = _dot(x, wgu_bf_ref[:, ucols]) + bgu_ref[:, ucols]
            gate = jnp.minimum(hg, SWIGLU_LIMIT)
            up = jnp.clip(hu, -SWIGLU_LIMIT, SWIGLU_LIMIT)
            act = (up + 1.0) * gate * _sigmoid(SWIGLU_ALPHA * gate)
            y = y + _dot(act.astype(BF16), wd_bf_ref[cols, :])
        _store_row_tiles(y_ref, y + bd_ref[...])


def _experts(tile_expert, n_used, xs, wgu, bgu, wd, bd, *, f_chunk):
    n_tiles = tile_expert.shape[0]
    _, d_tiles, _ = xs.shape
    d = d_tiles * LANES
    _, _, two_f = wgu.shape
    d_ff = two_f // 2
    by_expert = lambda r, c: pl.BlockSpec((None, r, c), lambda t, te, nu: (te[t], 0, 0))
    grid_spec = pltpu.PrefetchScalarGridSpec(
        num_scalar_prefetch=2, grid=(n_tiles,),
        in_specs=[pl.BlockSpec((EXPERT_TILE, d_tiles, LANES), lambda t, te, nu: (jnp.minimum(t, nu[0] - 1), 0, 0)),
                  by_expert(d, two_f), by_expert(1, two_f), by_expert(d_ff, d), by_expert(1, d)],
        out_specs=pl.BlockSpec((EXPERT_TILE, d_tiles, LANES), lambda t, te, nu: (t, 0, 0)),
        scratch_shapes=[pltpu.VMEM((d, two_f), BF16), pltpu.VMEM((d_ff, d), BF16)],
    )
    return pl.pallas_call(
        functools.partial(_experts_body, f_chunk=f_chunk), grid_spec=grid_spec,
        out_shape=jax.ShapeDtypeStruct((n_tiles * EXPERT_TILE, d_tiles, LANES), F32),
        compiler_params=_cparams(("arbitrary",)), name="experts",
    )(tile_expert, n_used, xs, wgu, bgu, wd, bd)


def _combine_body(pos_ref, y_ref, route_ref, x1_ref, g2_ref, out_ref, ybuf_ref, sem_ref, *, first_token, n_tokens):
    i = pl.program_id(0)
    slot = i % 2
    tc = out_ref.shape[0]

    def row_copy(pos, s, r, k):
        return pltpu.make_async_copy(y_ref.at[pos], ybuf_ref.at[s, k, r], sem_ref.at[s])

    def gather(tile, s):
        for k in range(TOP_K):
            for r in range(tc):
                row_copy(pos_ref[k * n_tokens + first_token + tile * tc + r], s, r, k).start(priority=r % 2)

    @pl.when(i == 0)
    def _():
        gather(0, 0)

    @pl.when(i + 1 < pl.num_programs(0))
    def _():
        gather(i + 1, 1 - slot)

    for k in range(TOP_K):
        for r in range(tc):
            row_copy(0, slot, r, k).wait()

    ff = jnp.zeros(out_ref.shape, F32)
    for k in range(TOP_K):
        ff = ff + route_ref[:, TOP_K + k:TOP_K + k + 1] * _load_row_tiles(ybuf_ref.at[slot, k])
    out_ref[...] = x1_ref[...] + g2_ref[...] * ff


def _combine(pos_flat, y, route, x1, g2, *, tc, rows_per_batch, per_row, first_token, n_tokens):
    n, d = x1.shape
    row = lambda i, pos: (i, 0)
    if per_row:
        mod_spec = pl.BlockSpec((tc, d), row)
    else:
        tiles_per_batch = rows_per_batch // tc
        mod_spec = pl.BlockSpec((None, 1, d), lambda i, pos: (i // tiles_per_batch, 0, 0))
    grid_spec = pltpu.PrefetchScalarGridSpec(
        num_scalar_prefetch=1, grid=(n // tc,),
        in_specs=[pl.BlockSpec(memory_space=pl.ANY), pl.BlockSpec((tc, LANES), row), pl.BlockSpec((tc, d), row),
                  mod_spec],
        out_specs=pl.BlockSpec((tc, d), row),
        scratch_shapes=[pltpu.VMEM((2, TOP_K, tc, d // LANES, LANES), F32), pltpu.SemaphoreType.DMA((2,))],
    )
    return pl.pallas_call(
        functools.partial(_combine_body, first_token=first_token, n_tokens=n_tokens), grid_spec=grid_spec,
        out_shape=jax.ShapeDtypeStruct((n, d), F32),
        compiler_params=_cparams(("arbitrary",)), name="combine",
    )(pos_flat, y, route, x1, g2)


def _dispatch_plan(route, counts, n_experts, n_tiles):
    n = route.shape[0]
    ids = route[:, :TOP_K].astype(I32)
    ranks = route[:, 2 * TOP_K:3 * TOP_K].astype(I32)
    cnt = counts[0, :n_experts].astype(I32)
    tiles_per_expert = (cnt + EXPERT_TILE - 1) // EXPERT_TILE
    tile_end = jnp.cumsum(tiles_per_expert)
    row_start = (tile_end - tiles_per_expert) * EXPERT_TILE
    pos = (row_start[ids] + ranks).T
    n_used = tile_end[-1]
    tile = jnp.arange(n_tiles, dtype=I32)
    last = jnp.minimum(tile, n_used - 1)
    tile_expert = jnp.sum((tile_end[None, :] <= last[:, None]).astype(I32), axis=1)
    return pos.reshape(-1), tile_expert, row_start + cnt, tile_end * EXPERT_TILE, n_used.reshape(1).astype(I32)


def _rope_tables(pos, d):
    pos = np.asarray(pos, np.float64)
    inv = np.power(ROPE_THETA, -np.arange(0, d, 2, dtype=np.float64) / d)
    ang = pos[:, None] * inv[None, :]
    cos = np.cos(np.concatenate([ang, ang], axis=-1))
    sin = np.sin(ang)
    sin = np.concatenate([-sin, sin], axis=-1)
    reps = LANES // d
    return jnp.asarray(np.tile(cos, (1, reps)), F32), jnp.asarray(np.tile(sin, (1, reps)), F32)


def _split_bf16(x):
    hi = x.astype(BF16)
    return hi, (x - hi.astype(F32)).astype(BF16)


def _adaln_body(c_ref, w_ref, b_ref, o_ref):
    c = c_ref[...]
    a_hi, a_lo = _split_bf16(c * _sigmoid(c))
    w_hi, w_lo = _split_bf16(w_ref[...])
    o_ref[...] = _dot(a_hi, w_hi) + _dot(a_lo, w_hi) + _dot(a_hi, w_lo) + b_ref[...]


def _adaln(c, ada_w, ada_b):
    n, d = c.shape
    n_out = ada_w.shape[1]
    tn = d
    out = pl.pallas_call(
        _adaln_body, grid=(n_out // tn,),
        in_specs=[pl.BlockSpec((n, d), lambda j: (0, 0)), pl.BlockSpec((d, tn), lambda j: (0, j)),
                  pl.BlockSpec((1, tn), lambda j: (0, j))],
        out_specs=pl.BlockSpec((n, tn), lambda j: (0, j)),
        out_shape=jax.ShapeDtypeStruct((n, n_out), F32),
        compiler_params=_cparams(("arbitrary",)), name="adaln",
    )(c, ada_w, ada_b[None])
    return jnp.split(out, 6, axis=-1)


def kernel(x_prompt, x_sample, cache_k, cache_v, cache_idx_k, state_ret, page_table, c_prompt, c_sample,
           ada_w, ada_b, norm1_w, w_in, q_norm_w, k_norm_w, idx_k_norm_w, ret_gn_w, w_out, norm2_w,
           router_w, router_b, w_gate_up, b_gate_up, w_down, b_down):
    depth = w_in.shape[0]
    b, t, d = x_prompt.shape
    db, ts, _ = x_sample.shape
    assert ts == 1, "the sample group decodes one token per sequence"
    _, _, h_ret, d_ret, _ = state_ret.shape
    _, n_pool, page, h_att, d_head = cache_k.shape
    assert page == PAGE_SIZE
    d_idx = cache_idx_k.shape[-1]
    ret_w, att_w = h_ret * d_ret, h_att * d_head
    assert ret_w == att_w == H_IDX * d_idx and d_ret == d_head == d_idx == 64
    n_experts = router_w.shape[-1]
    n_pages = page_table.shape[1]
    past = n_pages * PAGE_SIZE
    n_main = 8 * ret_w

    cos_p, sin_p = _rope_tables(np.arange(t), d_ret)
    cos_s, sin_s = _rope_tables(np.full((db,), past), d_ret)
    log_g = jnp.log1p(-jnp.power(2.0, -5.0 - jnp.arange(h_ret, dtype=F32)))
    gmat = jnp.asarray(np.kron(np.eye(ret_w // d_head), np.full((d_head, d_head), 1.0 / d_head)), BF16)

    hp, hs = x_prompt.reshape(b * t, d), x_sample.reshape(db, d)
    outs = [[] for _ in range(8)]
    for l in range(depth):
        w_main = w_in[l, :, :n_main]
        w_tail = jnp.pad(w_in[l, :, n_main:], ((0, 0), (0, LANES - (w_in.shape[2] - n_main))))
        qn = jnp.tile(q_norm_w[l], h_att)[None]
        kn = jnp.tile(k_norm_w[l], h_att)[None]
        ikn = jnp.pad(idx_k_norm_w[l], (0, LANES - d_idx))[None]
        gnw = ret_gn_w[l].reshape(h_ret, 1, d_ret)
        wo = w_out[l]
        rw = jnp.pad(router_w[l], ((0, 0), (0, LANES - n_experts)))
        rb = jnp.pad(router_b[l], (0, LANES - n_experts))[None]
        n1, n2 = norm1_w[l][None], norm2_w[l][None]
        bgu, bd = b_gate_up[l][:, None, :], b_down[l][:, None, :]
        inproj = functools.partial(_inproj, n1=n1, qn=qn, kn=kn, ikn=ikn, gmat=gmat,
                                   d_ret=d_ret, d_head=d_head, d_idx=d_idx)
        outproj = functools.partial(_outproj, n2=n2, rw=rw, rb=rb, n_experts=n_experts)

        mods = _adaln(jnp.concatenate([c_prompt, c_sample], axis=0), ada_w[l], ada_b[l])

        sh1, sc1, g1, sh2, sc2, g2 = [a[:b, None, :] for a in mods]
        rq, rk, rv, rg, aq, ak, akb, av, avb, iq, tail = inproj(
            hp, sc1, sh1, w_main=w_main.astype(BF16), w_tail=w_tail.astype(BF16), cos=cos_p, sin=sin_p,
            tm=256, rows_per_batch=t, per_row=False, precise=False)

        heads = lambda a, nh: a.reshape(b, t, nh, -1).transpose(0, 2, 1, 3)
        ret_o, ret_s = _ret_prompt(log_g, heads(rq, h_ret), heads(rk, h_ret).swapaxes(2, 3), heads(rv, h_ret),
                                   heads(rg, h_ret), gnw, chunk=256)
        ret_o = ret_o.transpose(0, 2, 1, 3).reshape(b * t, ret_w)

        tail3 = tail.reshape(b, t, LANES)
        ikp = tail3[:, :, :d_idx]
        seq_t = lambda a: a.reshape(b, t, -1).swapaxes(1, 2)
        iwt = tail3[:, :, d_idx:d_idx + H_IDX].swapaxes(1, 2) * (d_idx ** -0.5)
        att_t = _dsa_prompt(ikp.astype(BF16), seq_t(iq), iwt, seq_t(aq), heads(akb, h_att), seq_t(avb),
                            qb=128, k_top=min(TOPK_MAX, t // 4), n_splits=4)
        att_o = att_t.swapaxes(1, 2).reshape(b * t, att_w)

        x1_p, h2_p, route_p, counts = outproj(hp, ret_o, att_o, wo=wo.astype(BF16), g1=g1, sc2=sc2, sh2=sh2,
                                              cnt_in=jnp.zeros((1, LANES), F32), tm=512, rows_per_batch=t,
                                              per_row=False, precise=False)
        g2_p = g2
        outs[0].append(ak.reshape(b, t, h_att, d_head))
        outs[1].append(av.reshape(b, t, h_att, d_head))
        outs[2].append(ikp)
        outs[3].append(ret_s)

        sh1, sc1, g1, sh2, sc2, g2 = [a[b:] for a in mods]
        rq, rk, rv, rg, aq, ak, akb, av, avb, iq, tail = inproj(
            hs, sc1, sh1, w_main=w_main, w_tail=w_tail, cos=cos_s, sin=sin_s, tm=db, rows_per_batch=1,
            per_row=True, precise=True)
        hd = lambda a: a.reshape(db, h_ret, d_ret)
        ret_o, ret_s = _ret_sample(log_g, hd(rq), hd(rk), hd(rv), hd(rg), state_ret[l], gnw)

        iks = tail[:, :d_idx]
        iw = tail[:, d_idx:d_idx + H_IDX] * (d_idx ** -0.5)
        scores = _sample_index_scores(page_table, iq.reshape(db, H_IDX, d_idx), iw[:, :, None], iks[:, :, None],
                                      cache_idx_k[l].transpose(0, 2, 1))
        bias = _sample_select(scores, n_visible=past + ts, k_top=min(TOPK_MAX, (past + ts) // 4))
        hda = lambda a: a.reshape(db, h_att, d_head, 1)
        att_o = _sample_attention(page_table, hda(aq), bias, hda(ak), hda(av), cache_k[l].transpose(0, 2, 3, 1),
                                  cache_v[l].transpose(0, 2, 3, 1), group=8)

        x1_s, h2_s, route_s, counts = outproj(hs, ret_o, att_o.reshape(db, att_w), wo=wo, g1=g1, sc2=sc2, sh2=sh2,
                                              cnt_in=counts, tm=db, rows_per_batch=1, per_row=True, precise=True)

        n_all = b * t + db
        n_tiles = -(-TOP_K * n_all // EXPERT_TILE) + n_experts
        pos, tile_expert, pad_lo, pad_hi, n_used = _dispatch_plan(
            jnp.concatenate([route_p, route_s], axis=0), counts, n_experts, n_tiles)
        xs = _dispatch(pos, pad_lo, pad_hi, n_used, jnp.concatenate([h2_p, h2_s], axis=0), n_tiles * EXPERT_TILE)
        y = _experts(tile_expert, n_used, xs, w_gate_up[l], bgu, w_down[l], bd, f_chunk=512)
        hp = _combine(pos, y, route_p, x1_p, g2_p, tc=128, rows_per_batch=t, per_row=False, first_token=0,
                      n_tokens=n_all)
        hs = _combine(pos, y, route_s, x1_s, g2, tc=db, rows_per_batch=1, per_row=True, first_token=b * t,
                      n_tokens=n_all)
        outs[4].append(ak.reshape(db, ts, h_att, d_head))
        outs[5].append(av.reshape(db, ts, h_att, d_head))
        outs[6].append(iks.reshape(db, ts, d_idx))
        outs[7].append(ret_s)

    k_p, v_p, ik_p, r_p, k_s, v_s, ik_s, r_s = [jnp.stack(o) for o in outs]
    return (hp.reshape(b, t, d), hs.reshape(db, ts, d), k_p, v_p, ik_p, r_p, k_s, v_s, ik_s, r_s)
```
